```python
import math
import jax, jax.numpy as jnp
from jax import lax
import numpy as np

D_MODEL = 1024
BATCH = 4
SEQ = 4096
DEPTH = 2

MEM_LEN = 256
EPS = 1e-6
SB_HEADS = 8
SB_HEAD_DIM = 64
SB_WIDTH = SB_HEADS * SB_HEAD_DIM
SB_BLOCK = 128
GLA_HEADS = 4
GLA_DK = 64
GLA_DV = 128
GLA_KW = GLA_HEADS * GLA_DK
GLA_VW = GLA_HEADS * GLA_DV
GLA_GATE_RANK = 16
GLA_GATE_TAU = 16.0
GLA_CHUNK = 64
CA_HEADS = 4
CA_HEAD_DIM = 128
CA_WIDTH = CA_HEADS * CA_HEAD_DIM
N_BRANCH = 3
IN_SPLITS = (SB_WIDTH, SB_WIDTH, SB_WIDTH, GLA_KW, GLA_KW, GLA_VW, GLA_GATE_RANK, GLA_VW, CA_WIDTH, N_BRANCH * D_MODEL)
IN_WIDTH = 3 * SB_WIDTH + 2 * GLA_KW + 2 * GLA_VW + GLA_GATE_RANK + CA_WIDTH + N_BRANCH * D_MODEL
D_FF = 2816
N_EXPERTS = 8
TOP_K = 2
D_FF_EXPERT = 3584
N_DENSE = (DEPTH + 1) // 2
N_MOE = DEPTH // 2

kernel_name = 'hybrid_sb_gla_memxattn_moe_trunk'


def rms_norm(x, g):
    xf = x.astype(jnp.float32)
    y = xf * lax.rsqrt(jnp.mean(xf * xf, axis=-1, keepdims=True) + EPS)
    return (y * g.astype(jnp.float32)).astype(x.dtype)


def split_heads(t, n):
    b, s, w = t.shape
    return t.reshape(b, s, n, w // n).transpose(0, 2, 1, 3)


def merge_heads(t):
    b, h, s, d = t.shape
    return t.transpose(0, 2, 1, 3).reshape(b, s, h * d)


def stick_breaking_attention(q, k, v):
    s_len, d = q.shape[2], q.shape[3]
    scale = d ** -0.5
    outs = []
    for i in range(s_len // SB_BLOCK):
        q0 = i * SB_BLOCK
        q1 = q0 + SB_BLOCK
        qb = q[:, :, q0:q1].astype(jnp.float32)
        kb = k[:, :, :q1].astype(jnp.float32)
        z = jnp.einsum('bhtd,bhsd->bhts', qb, kb) * scale
        causal = jnp.arange(q1)[None, :] < jnp.arange(q0, q1)[:, None]
        log_beta = jax.nn.log_sigmoid(z)
        log_1m_beta = jnp.where(causal, log_beta - z, 0.0)
        tail = lax.cumsum(log_1m_beta, axis=3, reverse=True) - log_1m_beta
        w = jnp.where(causal, jnp.exp(log_beta + tail), 0.0)
        outs.append(jnp.einsum('bhts,bhsd->bhtd', w.astype(v.dtype), v[:, :, :q1]))
    return jnp.concatenate(outs, axis=2)


def gla_chunked(q, k, v, log_a):
    b_, h_, s_len, dk = q.shape
    dv = v.shape[-1]
    c = GLA_CHUNK
    n = s_len // c
    qf = q.astype(jnp.float32).reshape(b_, h_, n, c, dk) * (dk ** -0.5)
    kf = k.astype(jnp.float32).reshape(b_, h_, n, c, dk)
    vf = v.astype(jnp.float32).reshape(b_, h_, n, c, dv)
    cum = jnp.cumsum(log_a.astype(jnp.float32).reshape(b_, h_, n, c, dk), axis=3)
    cum_last = cum[:, :, :, -1:, :]
    q_dec = qf * jnp.exp(cum)
    k_inv = kf * jnp.exp(-cum)
    k_end = kf * jnp.exp(cum_last - cum)
    tri = jnp.tril(jnp.ones((c, c), dtype=bool))
    scores = jnp.where(tri, jnp.einsum('bhnid,bhnjd->bhnij', q_dec, k_inv), 0.0)
    o_intra = jnp.einsum('bhnij,bhnjv->bhniv', scores, vf)
    kv_chunk = jnp.einsum('bhnjd,bhnjv->bhndv', k_end, vf)
    decay_chunk = jnp.exp(cum_last[:, :, :, 0, :])

    def step(state, inp):
        dec, kv = inp
        return dec[..., None] * state + kv, state

    _, states = lax.scan(step, jnp.zeros((b_, h_, dk, dv), jnp.float32),
                         (jnp.moveaxis(decay_chunk, 2, 0), jnp.moveaxis(kv_chunk, 2, 0)))
    states = jnp.moveaxis(states, 0, 2)
    o_inter = jnp.einsum('bhnid,bhndv->bhniv', q_dec, states)
    return (o_intra + o_inter).reshape(b_, h_, s_len, dv).astype(v.dtype)


def memory_cross_attention(q, mk, mv):
    d = q.shape[-1]
    s = jnp.einsum('bhtd,bhmd->bhtm', q.astype(jnp.float32), mk.astype(jnp.float32)) * (d ** -0.5)
    p = jax.nn.softmax(s, axis=-1)
    return jnp.einsum('bhtm,bhmd->bhtd', p.astype(mv.dtype), mv)


def hybrid_mixer(x, mem, norm_g, w_in, sb_q_g, sb_k_g, gla_a_up, gla_a_b, gla_o_g, ca_q_g, ca_k_g,
                 mem_norm_g, w_mem_kv, w_br_sb, w_br_gla, w_br_ca, w_out):
    b_, s_len, dm = x.shape
    h = rms_norm(x, norm_g)
    proj = h @ w_in
    offsets = np.cumsum(IN_SPLITS)[:-1].tolist()
    sb_q, sb_k, sb_v, g_q, g_k, g_v, g_a, g_r, ca_q, gates = jnp.split(proj, offsets, axis=-1)

    q_a = rms_norm(split_heads(sb_q, SB_HEADS), sb_q_g)
    k_a = rms_norm(split_heads(sb_k, SB_HEADS), sb_k_g)
    o_a = merge_heads(stick_breaking_attention(q_a, k_a, split_heads(sb_v, SB_HEADS)))

    log_a = jax.nn.log_sigmoid((g_a @ gla_a_up + gla_a_b).astype(jnp.float32)) / GLA_GATE_TAU
    o_b = gla_chunked(split_heads(g_q, GLA_HEADS), split_heads(g_k, GLA_HEADS),
                      split_heads(g_v, GLA_HEADS), split_heads(log_a, GLA_HEADS))
    o_b = merge_heads(rms_norm(o_b, gla_o_g)) * jax.nn.silu(g_r)

    mkv = rms_norm(mem, mem_norm_g) @ w_mem_kv
    m_k, m_v = jnp.split(mkv, 2, axis=-1)
    q_c = rms_norm(split_heads(ca_q, CA_HEADS), ca_q_g)
    k_c = rms_norm(split_heads(m_k, CA_HEADS), ca_k_g)
    o_c = merge_heads(memory_cross_attention(q_c, k_c, split_heads(m_v, CA_HEADS)))

    g = jax.nn.sigmoid(gates.astype(jnp.float32)).astype(x.dtype).reshape(b_, s_len, N_BRANCH, dm)
    merged = g[:, :, 0] * (o_a @ w_br_sb) + g[:, :, 1] * (o_b @ w_br_gla) + g[:, :, 2] * (o_c @ w_br_ca)
    return merged @ w_out


def swiglu(h, w_gate_up, w_down):
    gt, up = jnp.split(h @ w_gate_up, 2, axis=-1)
    return (jax.nn.silu(gt) * up) @ w_down


def moe_swiglu(h, w_router, w_gate_up, w_down):
    b_, s_len, dm = h.shape
    t = h.reshape(-1, dm)
    logits = (t @ w_router).astype(jnp.float32)
    top_val, top_idx = lax.top_k(logits, TOP_K)
    top_w = jax.nn.softmax(top_val, axis=-1)
    combine = jnp.sum(jax.nn.one_hot(top_idx, N_EXPERTS, dtype=jnp.float32) * top_w[..., None], axis=1)
    combine = combine.astype(t.dtype)
    y = jnp.zeros_like(t)
    for e in range(N_EXPERTS):
        y = y + combine[:, e:e + 1] * swiglu(t, w_gate_up[e], w_down[e])
    return y.reshape(b_, s_len, dm)


def setup_inputs(seed: int = 0) -> dict:
    key = jax.random.key(seed)
    ks = jax.random.split(key, 24)
    f32 = jnp.float32
    L = DEPTH

    def nrm(k, shape, fan_in):
        return jax.random.normal(k, shape, f32) * (fan_in ** -0.5)

    def gain(k, shape):
        return 1.0 + 0.1 * jax.random.normal(k, shape, f32)

    return {
        'x': jax.random.normal(ks[0], (BATCH, SEQ, D_MODEL), f32),
        'mem': jax.random.normal(ks[1], (BATCH, MEM_LEN, D_MODEL), f32),
        'mix_norm_g': gain(ks[2], (L, D_MODEL)),
        'w_in': nrm(ks[3], (L, D_MODEL, IN_WIDTH), D_MODEL),
        'sb_q_g': gain(ks[4], (L, SB_HEAD_DIM)),
        'sb_k_g': gain(ks[5], (L, SB_HEAD_DIM)),
        'gla_a_up': nrm(ks[6], (L, GLA_GATE_RANK, GLA_KW), GLA_GATE_RANK),
        'gla_a_b': 0.1 * jax.random.normal(ks[7], (L, GLA_KW), f32),
        'gla_o_g': gain(ks[8], (L, GLA_DV)),
        'ca_q_g': gain(ks[9], (L, CA_HEAD_DIM)),
        'ca_k_g': gain(ks[10], (L, CA_HEAD_DIM)),
        'mem_norm_g': gain(ks[11], (L, D_MODEL)),
        'w_mem_kv': nrm(ks[12], (L, D_MODEL, 2 * CA_WIDTH), D_MODEL),
        'w_br_sb': nrm(ks[13], (L, SB_WIDTH, D_MODEL), SB_WIDTH),
        'w_br_gla': nrm(ks[14], (L, GLA_VW, D_MODEL), GLA_VW),
        'w_br_ca': nrm(ks[15], (L, CA_WIDTH, D_MODEL), CA_WIDTH),
        'w_out': nrm(ks[16], (L, D_MODEL, D_MODEL), D_MODEL),
        'ffn_norm_g': gain(ks[17], (L, D_MODEL)),
        'ff_w_gate_up': nrm(ks[18], (N_DENSE, D_MODEL, 2 * D_FF), D_MODEL),
        'ff_w_down': nrm(ks[19], (N_DENSE, D_FF, D_MODEL), D_FF),
        'moe_w_router': nrm(ks[20], (N_MOE, D_MODEL, N_EXPERTS), D_MODEL),
        'moe_w_gate_up': nrm(ks[21], (N_MOE, N_EXPERTS, D_MODEL, 2 * D_FF_EXPERT), D_MODEL),
        'moe_w_down': nrm(ks[22], (N_MOE, N_EXPERTS, D_FF_EXPERT, D_MODEL), D_FF_EXPERT),
    }


def reference(x, mem, mix_norm_g, w_in, sb_q_g, sb_k_g, gla_a_up, gla_a_b, gla_o_g, ca_q_g, ca_k_g,
              mem_norm_g, w_mem_kv, w_br_sb, w_br_gla, w_br_ca, w_out, ffn_norm_g,
              ff_w_gate_up, ff_w_down, moe_w_router, moe_w_gate_up, moe_w_down):
    for layer in range(DEPTH):
        x = x + hybrid_mixer(x, mem, mix_norm_g[layer], w_in[layer], sb_q_g[layer], sb_k_g[layer],
                             gla_a_up[layer], gla_a_b[layer], gla_o_g[layer], ca_q_g[layer], ca_k_g[layer],
                             mem_norm_g[layer], w_mem_kv[layer], w_br_sb[layer], w_br_gla[layer],
                             w_br_ca[layer], w_out[layer])
        h = rms_norm(x, ffn_norm_g[layer])
        j = layer // 2
        if layer % 2 == 0:
            x = x + swiglu(h, ff_w_gate_up[j], ff_w_down[j])
        else:
            x = x + moe_swiglu(h, moe_w_router[j], moe_w_gate_up[j], moe_w_down[j])
    return x
```

```python
import functools

import jax
import jax.numpy as jnp
from jax import lax
from jax.experimental import pallas as pl
from jax.experimental.pallas import tpu as pltpu

F32 = jnp.float32
BF16 = jnp.bfloat16

LANES = 128
EPS = 1e-6
VMEM_LIMIT_BYTES = 56 * 1024 * 1024

D_MODEL = 1024
SB_HEADS, SB_HEAD_DIM = 8, 64
GLA_HEADS, GLA_DK, GLA_DV = 4, 64, 128
GLA_GATE_RANK, GLA_GATE_TAU, GLA_CHUNK = 16, 16.0, 64
CA_HEADS, CA_HEAD_DIM = 4, 128
N_BRANCH = 3
N_EXPERTS, TOP_K = 8, 2

SLAB_SB_Q, SLAB_SB_K, SLAB_SB_V = 0, 4, 8
SLAB_GLA_V, SLAB_GLA_R = 12, 16
SLAB_CA_Q = 20
SLAB_GATES = 24
SLAB_GLA_Q, SLAB_GLA_K, SLAB_GLA_A = 48, 50, 52
N_SLABS = 54


def _params(*semantics):
    return pltpu.CompilerParams(dimension_semantics=semantics, vmem_limit_bytes=VMEM_LIMIT_BYTES)


def _rms_rows(x, eps=EPS):
    return x * lax.rsqrt(jnp.mean(x * x, axis=-1, keepdims=True) + eps)


def _rms_half_lanes(x, eps=EPS):
    lane = lax.broadcasted_iota(jnp.int32, x.shape, 1)
    lo = lane < SB_HEAD_DIM
    sq = x * x
    s_lo = jnp.sum(jnp.where(lo, sq, 0.0), axis=-1, keepdims=True)
    s_hi = jnp.sum(jnp.where(lo, 0.0, sq), axis=-1, keepdims=True)
    ms = jnp.where(lo, s_lo, s_hi) * (1.0 / SB_HEAD_DIM)
    return x * lax.rsqrt(ms + eps)


def _split_bf16(x):
    hi = x.astype(BF16)
    lo = (x - hi.astype(F32)).astype(BF16)
    return hi, lo


def _norm_matmul_kernel(x_ref, g_ref, w_ref, o_ref, xn_ref, *, n_slab):
    @pl.when(pl.program_id(1) == 0)
    def _():
        xn_ref[...] = (_rms_rows(x_ref[...]) * g_ref[...]).astype(BF16)

    acc = jnp.dot(xn_ref[...], w_ref[...], preferred_element_type=F32)
    if n_slab is None:
        o_ref[...] = acc.astype(o_ref.dtype)
    else:
        for s in range(n_slab):
            o_ref[s] = acc[:, s * LANES:(s + 1) * LANES].astype(o_ref.dtype)


def norm_matmul(x, g, w, *, tm, tn, slabs):
    n, k = x.shape
    m = w.shape[1]
    tm = min(tm, n)
    tn = min(tn, m)
    assert n % tm == 0 and m % tn == 0 and tn % LANES == 0
    if slabs:
        n_slab = tn // LANES
        out_shape = jax.ShapeDtypeStruct((m // LANES, n, LANES), BF16)
        out_spec = pl.BlockSpec((n_slab, tm, LANES), lambda i, j: (j, i, 0))
    else:
        n_slab = None
        out_shape = jax.ShapeDtypeStruct((n, m), BF16)
        out_spec = pl.BlockSpec((tm, tn), lambda i, j: (i, j))
    return pl.pallas_call(
        functools.partial(_norm_matmul_kernel, n_slab=n_slab),
        grid=(n // tm, m // tn),
        in_specs=[
            pl.BlockSpec((tm, k), lambda i, j: (i, 0)),
            pl.BlockSpec((1, k), lambda i, j: (0, 0)),
            pl.BlockSpec((k, tn), lambda i, j: (0, j)),
        ],
        out_specs=out_spec,
        out_shape=out_shape,
        scratch_shapes=[pltpu.VMEM((tm, k), BF16)],
        compiler_params=_params("parallel", "arbitrary"),
        name="norm_matmul",
    )(x, g.reshape(1, k), w)


def _softplus(z):
    return jnp.maximum(z, 0.0) + jnp.log(1.0 + jnp.exp(-jnp.abs(z)))


def _sb_kernel(q_ref, k_ref, v_ref, qg_ref, kg_ref, o_ref, kn_ref, *, tq, tk, seq):
    i = pl.program_id(2)
    rows_per_norm = min(512, seq)

    @pl.when(i == 0)
    def _():
        def norm_chunk(c, carry):
            r0 = pl.multiple_of(c * rows_per_norm, rows_per_norm)
            kk = k_ref[0, pl.ds(r0, rows_per_norm), :].astype(F32)
            kn_ref[pl.ds(r0, rows_per_norm), :] = (_rms_half_lanes(kk) * kg_ref[...]).astype(BF16)
            return carry
        lax.fori_loop(0, seq // rows_per_norm, norm_chunk, 0)

    scale = SB_HEAD_DIM ** -0.5
    qn = _rms_half_lanes(q_ref[0].astype(F32)) * (qg_ref[...] * scale)
    lane = lax.broadcasted_iota(jnp.int32, (tq, LANES), 1)
    head_lanes = (lane < SB_HEAD_DIM, lane >= SB_HEAD_DIM)

    rj = lax.broadcasted_iota(jnp.int32, (tk, tk), 0)
    cs = lax.broadcasted_iota(jnp.int32, (tk, tk), 1)
    u_neg = jnp.where(rj >= cs, -1.0, 0.0).astype(BF16)

    t_idx = lax.broadcasted_iota(jnp.int32, (tq, tk), 0)
    s_idx = lax.broadcasted_iota(jnp.int32, (tq, tk), 1)
    n_diag = tq // tk

    def key_block(qh, j, carry, acc, diag_off):
        k0 = pl.multiple_of(j * tk, tk)
        kb = kn_ref[pl.ds(k0, tk), :]
        vb = v_ref[0, pl.ds(k0, tk), :]
        z = lax.dot_general(qh, kb, (((1,), (1,)), ((), ())), preferred_element_type=F32)
        sp = _softplus(z)
        if diag_off is not None:
            causal = (s_idx + diag_off) < t_idx
            sp = jnp.where(causal, sp, 0.0)
        hi, lo = _split_bf16(sp)
        t = (jnp.dot(hi, u_neg, preferred_element_type=F32)
             + jnp.dot(lo, u_neg, preferred_element_type=F32))
        w = jnp.exp(z + t + carry)
        if diag_off is not None:
            w = jnp.where(causal, w, 0.0)
        acc = acc + jnp.dot(w.astype(BF16), vb, preferred_element_type=F32)
        return carry + t[:, 0:1], acc

    outs = []
    for h in range(2):
        qh = jnp.where(head_lanes[h], qn, 0.0).astype(BF16)
        carry = jnp.zeros((tq, 1), F32)
        acc = jnp.zeros((tq, LANES), F32)
        for d in reversed(range(n_diag)):
            carry, acc = key_block(qh, i * n_diag + d, carry, acc, d * tk)

        def body(n, ca):
            return key_block(qh, i * n_diag - 1 - n, ca[0], ca[1], None)
        carry, acc = lax.fori_loop(0, i * n_diag, body, (carry, acc))
        outs.append(acc)
    o_ref[0] = jnp.where(head_lanes[0], outs[0], outs[1]).astype(o_ref.dtype)


def sb_attention(proj, q_gain, k_gain, *, batch, seq, tq, tk):
    tq = min(tq, seq)
    tk = min(tk, tq)
    assert seq % tq == 0 and tq % tk == 0
    nq = seq // tq
    n_pairs = SB_HEADS // 2
    g2 = lambda g: jnp.concatenate([g, g]).reshape(1, LANES).astype(F32)
    return pl.pallas_call(
        functools.partial(_sb_kernel, tq=tq, tk=tk, seq=seq),
        grid=(batch, n_pairs, nq),
        in_specs=[
            pl.BlockSpec((1, tq, LANES), lambda b, p, i: (SLAB_SB_Q + p, b * nq + i, 0)),
            pl.BlockSpec((1, seq, LANES), lambda b, p, i: (SLAB_SB_K + p, b, 0)),
            pl.BlockSpec((1, seq, LANES), lambda b, p, i: (SLAB_SB_V + p, b, 0)),
            pl.BlockSpec((1, LANES), lambda b, p, i: (0, 0)),
            pl.BlockSpec((1, LANES), lambda b, p, i: (0, 0)),
        ],
        out_specs=pl.BlockSpec((1, tq, LANES), lambda b, p, i: (p, b * nq + i, 0)),
        out_shape=jax.ShapeDtypeStruct((n_pairs, batch * seq, LANES), BF16),
        scratch_shapes=[pltpu.VMEM((seq, LANES), BF16)],
        compiler_params=_params("parallel", "parallel", "arbitrary"),
        name="sb_attention",
    )(proj, proj, proj, g2(q_gain), g2(k_gain))


GLA_ROWS = 256
GLA_KW = GLA_HEADS * GLA_DK


def _gla_kernel(q_ref, k_ref, v_ref, a_ref, r_ref, aup_ref, ab_ref, og_ref, o_ref, state_ref):
    rows = GLA_ROWS
    n_chunks = rows // GLA_CHUNK

    @pl.when(pl.program_id(1) == 0)
    def _():
        state_ref[...] = jnp.zeros_like(state_ref)

    pre = jnp.dot(a_ref[0], aup_ref[...], preferred_element_type=F32) + ab_ref[...]
    log_a = -_softplus(-pre) * (1.0 / GLA_GATE_TAU)
    la_hi, la_lo = _split_bf16(log_a)

    ri = lax.broadcasted_iota(jnp.int32, (rows, rows), 0)
    ci = lax.broadcasted_iota(jnp.int32, (rows, rows), 1)
    same_chunk = (ri // GLA_CHUNK) == (ci // GLA_CHUNK)
    tri = same_chunk & (ri >= ci)
    tri_b = jnp.where(tri, 1.0, 0.0).astype(BF16)
    ones_b = jnp.where(same_chunk, 1.0, 0.0).astype(BF16)
    cum = (jnp.dot(tri_b, la_hi, preferred_element_type=F32)
           + jnp.dot(tri_b, la_lo, preferred_element_type=F32))
    cum_last = (jnp.dot(ones_b, la_hi, preferred_element_type=F32)
                + jnp.dot(ones_b, la_lo, preferred_element_type=F32))

    q_all = jnp.concatenate([q_ref[0], q_ref[1]], axis=-1).astype(F32)
    k_all = jnp.concatenate([k_ref[0], k_ref[1]], axis=-1).astype(F32)
    q_dec = q_all * (GLA_DK ** -0.5) * jnp.exp(cum)
    k_inv = (k_all * jnp.exp(-cum)).astype(BF16)
    k_end = (k_all * jnp.exp(cum_last - cum)).astype(BF16)
    v_all = jnp.concatenate([v_ref[h] for h in range(GLA_HEADS)], axis=-1)

    lane = lax.broadcasted_iota(jnp.int32, (rows, GLA_KW), 1)
    q_heads = [jnp.where((lane // GLA_DK) == h, q_dec, 0.0).astype(BF16) for h in range(GLA_HEADS)]

    o_heads = []
    for h in range(GLA_HEADS):
        sc = lax.dot_general(q_heads[h], k_inv, (((1,), (1,)), ((), ())), preferred_element_type=F32)
        sc = jnp.where(tri, sc, 0.0).astype(BF16)
        o_heads.append(jnp.dot(sc, v_ref[h], preferred_element_type=F32))

    ones_cols = jnp.ones((GLA_CHUNK, LANES), BF16)
    state = state_ref[...]
    inter = [[] for _ in range(GLA_HEADS)]
    for c in range(n_chunks):
        r0, r1 = c * GLA_CHUNK, (c + 1) * GLA_CHUNK
        state_b = state.astype(BF16)
        for h in range(GLA_HEADS):
            inter[h].append(jnp.dot(q_heads[h][r0:r1], state_b, preferred_element_type=F32))
        kv = lax.dot_general(k_end[r0:r1], v_all[r0:r1], (((0,), (0,)), ((), ())),
                             preferred_element_type=F32)
        kv_sel = jnp.concatenate(
            [kv[h * GLA_DK:(h + 1) * GLA_DK, h * GLA_DV:(h + 1) * GLA_DV] for h in range(GLA_HEADS)], axis=0)
        tot = (lax.dot_general(la_hi[r0:r1], ones_cols, (((0,), (0,)), ((), ())), preferred_element_type=F32)
               + lax.dot_general(la_lo[r0:r1], ones_cols, (((0,), (0,)), ((), ())), preferred_element_type=F32))
        state = jnp.exp(tot) * state + kv_sel
    state_ref[...] = state

    for h in range(GLA_HEADS):
        o = o_heads[h] + jnp.concatenate(inter[h], axis=0)
        o = _rms_rows(o) * og_ref[...]
        r = r_ref[h].astype(F32)
        o_ref[h] = (o * (r * jax.nn.sigmoid(r))).astype(o_ref.dtype)


def gla(proj, a_up, a_b, o_gain, *, batch, seq):
    rows = GLA_ROWS
    assert seq % rows == 0
    nb = seq // rows
    a_up_p = jnp.zeros((LANES, GLA_KW), BF16).at[:GLA_GATE_RANK].set(a_up.astype(BF16))
    row_blk = lambda n, s0: pl.BlockSpec((n, rows, LANES), lambda b, i: (s0 // n, b * nb + i, 0))
    return pl.pallas_call(
        _gla_kernel,
        grid=(batch, nb),
        in_specs=[
            row_blk(2, SLAB_GLA_Q), row_blk(2, SLAB_GLA_K), row_blk(4, SLAB_GLA_V),
            row_blk(1, SLAB_GLA_A), row_blk(4, SLAB_GLA_R),
            pl.BlockSpec((LANES, GLA_KW), lambda b, i: (0, 0)),
            pl.BlockSpec((1, GLA_KW), lambda b, i: (0, 0)),
            pl.BlockSpec((1, GLA_DV), lambda b, i: (0, 0)),
        ],
        out_specs=pl.BlockSpec((GLA_HEADS, rows, LANES), lambda b, i: (0, b * nb + i, 0)),
        out_shape=jax.ShapeDtypeStruct((GLA_HEADS, batch * seq, LANES), BF16),
        scratch_shapes=[pltpu.VMEM((GLA_KW, GLA_DV), F32)],
        compiler_params=_params("parallel", "arbitrary"),
        name="gla",
    )(proj, proj, proj, proj, proj, a_up_p,
      a_b.reshape(1, GLA_KW).astype(F32), o_gain.reshape(1, GLA_DV).astype(F32))


def _merge_kernel(x_ref, oa_ref, ob_ref, cq_ref, gate_ref, mkv_ref, cqg_ref, ckg_ref,
                  wsb_ref, wgla_ref, wca_ref, wout_ref, o_ref):
    ca_w = CA_HEADS * CA_HEAD_DIM
    oc = []
    for h in range(CA_HEADS):
        q = _rms_rows(cq_ref[h].astype(F32)) * (cqg_ref[...] * CA_HEAD_DIM ** -0.5)
        kk = _rms_rows(mkv_ref[:, h * CA_HEAD_DIM:(h + 1) * CA_HEAD_DIM].astype(F32)) * ckg_ref[...]
        s = lax.dot_general(q.astype(BF16), kk.astype(BF16), (((1,), (1,)), ((), ())),
                            preferred_element_type=F32)
        e = jnp.exp(s - jnp.max(s, axis=-1, keepdims=True))
        p = e * (1.0 / jnp.sum(e, axis=-1, keepdims=True))
        vv = mkv_ref[:, ca_w + h * CA_HEAD_DIM: ca_w + (h + 1) * CA_HEAD_DIM]
        oc.append(jnp.dot(p.astype(BF16), vv, preferred_element_type=F32).astype(BF16))
    o_c = jnp.concatenate(oc, axis=-1)
    o_a = jnp.concatenate([oa_ref[s] for s in range(4)], axis=-1)
    o_b = jnp.concatenate([ob_ref[s] for s in range(4)], axis=-1)

    def gate(b):
        g = jnp.concatenate([gate_ref[b * 8 + s] for s in range(8)], axis=-1).astype(F32)
        return jax.nn.sigmoid(g)

    merged = gate(0) * jnp.dot(o_a, wsb_ref[...], preferred_element_type=F32)
    merged += gate(1) * jnp.dot(o_b, wgla_ref[...], preferred_element_type=F32)
    merged += gate(2) * jnp.dot(o_c, wca_ref[...], preferred_element_type=F32)
    o_ref[...] = x_ref[...] + jnp.dot(merged.astype(BF16), wout_ref[...], preferred_element_type=F32)


def merge_branches(x, o_a, o_b, proj, mkv, cq_gain, ck_gain, w_sb, w_gla, w_ca, w_out, *, batch, seq, tm):
    n, d = x.shape
    tm = min(tm, seq)
    assert seq % tm == 0
    nt = seq // tm
    mem_len = mkv.shape[0] // batch
    full = lambda a: pl.BlockSpec(a.shape, lambda i: (0,) * a.ndim)
    cqg = cq_gain.reshape(1, CA_HEAD_DIM).astype(F32)
    ckg = ck_gain.reshape(1, CA_HEAD_DIM).astype(F32)
    return pl.pallas_call(
        _merge_kernel,
        grid=(n // tm,),
        in_specs=[
            pl.BlockSpec((tm, d), lambda i: (i, 0)),
            pl.BlockSpec((4, tm, LANES), lambda i: (0, i, 0)),
            pl.BlockSpec((4, tm, LANES), lambda i: (0, i, 0)),
            pl.BlockSpec((4, tm, LANES), lambda i: (SLAB_CA_Q // 4, i, 0)),
            pl.BlockSpec((24, tm, LANES), lambda i: (SLAB_GATES // 24, i, 0)),
            pl.BlockSpec((mem_len, mkv.shape[1]), lambda i: (i // nt, 0)),
            full(cqg), full(ckg), full(w_sb), full(w_gla), full(w_ca), full(w_out),
        ],
        out_specs=pl.BlockSpec((tm, d), lambda i: (i, 0)),
        out_shape=jax.ShapeDtypeStruct((n, d), F32),
        compiler_params=_params("parallel"),
        name="merge_branches",
    )(x, o_a, o_b, proj, proj, mkv,
      cqg, ckg, w_sb, w_gla, w_ca, w_out)


def _ffn_kernel(x_ref, g_ref, wg_ref, wu_ref, wd_ref, o_ref, hn_ref, acc_ref):
    f = pl.program_id(1)

    @pl.when(f == 0)
    def _():
        hn_ref[...] = (_rms_rows(x_ref[...]) * g_ref[...]).astype(BF16)
        acc_ref[...] = jnp.zeros_like(acc_ref)

    h = hn_ref[...]
    gt = jnp.dot(h, wg_ref[...], preferred_element_type=F32)
    up = jnp.dot(h, wu_ref[...], preferred_element_type=F32)
    act = (gt * jax.nn.sigmoid(gt) * up).astype(BF16)
    acc_ref[...] += jnp.dot(act, wd_ref[...], preferred_element_type=F32)

    @pl.when(f == pl.num_programs(1) - 1)
    def _():
        o_ref[...] = x_ref[...] + acc_ref[...]


def ffn_dense(x, g, w_gate_up, w_down, *, tm, tf):
    n, d = x.shape
    d_ff = w_down.shape[0]
    tm = min(tm, n)
    assert n % tm == 0 and d_ff % tf == 0 and tf % LANES == 0
    nf = d_ff // tf
    return pl.pallas_call(
        _ffn_kernel,
        grid=(n // tm, nf),
        in_specs=[
            pl.BlockSpec((tm, d), lambda i, f: (i, 0)),
            pl.BlockSpec((1, d), lambda i, f: (0, 0)),
            pl.BlockSpec((d, tf), lambda i, f: (0, f)),
            pl.BlockSpec((d, tf), lambda i, f: (0, nf + f)),
            pl.BlockSpec((tf, d), lambda i, f: (f, 0)),
        ],
        out_specs=pl.BlockSpec((tm, d), lambda i, f: (i, 0)),
        out_shape=jax.ShapeDtypeStruct((n, d), F32),
        scratch_shapes=[pltpu.VMEM((tm, d), BF16), pltpu.VMEM((tm, d), F32)],
        compiler_params=_params("parallel", "arbitrary"),
        name="ffn_dense",
    )(x, g.reshape(1, d), w_gate_up, w_gate_up, w_down)


def _router_kernel(x_ref, g_ref, wr_ref, c_ref):
    hn = _rms_rows(x_ref[...]) * g_ref[...]
    logits = jnp.dot(hn, wr_ref[...], preferred_element_type=F32, precision=lax.Precision.HIGHEST)
    lane = lax.broadcasted_iota(jnp.int32, logits.shape, 1).astype(F32)
    neg = jnp.float32(-jnp.inf)
    logits = jnp.where(lane < N_EXPERTS, logits, neg)
    m1 = jnp.max(logits, axis=-1, keepdims=True)
    i1 = jnp.min(jnp.where(logits == m1, lane, float(LANES)), axis=-1, keepdims=True)
    rest = jnp.where(lane == i1, neg, logits)
    m2 = jnp.max(rest, axis=-1, keepdims=True)
    i2 = jnp.min(jnp.where(rest == m2, lane, float(LANES)), axis=-1, keepdims=True)
    e2 = jnp.exp(m2 - m1)
    inv = 1.0 / (1.0 + e2)
    c_ref[...] = jnp.where(lane == i1, inv, 0.0) + jnp.where(lane == i2, e2 * inv, 0.0)


def router(x, g, w_router, *, tm):
    n, d = x.shape
    tm = min(tm, n)
    wr = jnp.zeros((d, LANES), F32).at[:, :N_EXPERTS].set(w_router)
    return pl.pallas_call(
        _router_kernel,
        grid=(n // tm,),
        in_specs=[
            pl.BlockSpec((tm, d), lambda i: (i, 0)),
            pl.BlockSpec((1, d), lambda i: (0, 0)),
            pl.BlockSpec((d, LANES), lambda i: (0, 0)),
        ],
        out_specs=pl.BlockSpec((tm, LANES), lambda i: (i, 0)),
        out_shape=jax.ShapeDtypeStruct((n, LANES), F32),
        compiler_params=_params("parallel"),
        name="router",
    )(x, g.reshape(1, d), wr)


def _moe_kernel(x_ref, g_ref, c_ref, wg_ref, wu_ref, wd_ref, o_ref, hn_ref, acc_ref):
    e = pl.program_id(1)
    f = pl.program_id(2)

    @pl.when((e == 0) & (f == 0))
    def _():
        hn_ref[...] = (_rms_rows(x_ref[...]) * g_ref[...]).astype(BF16)
        acc_ref[...] = jnp.zeros_like(acc_ref)

    h = hn_ref[...]
    gt = jnp.dot(h, wg_ref[0], preferred_element_type=F32)
    up = jnp.dot(h, wu_ref[0], preferred_element_type=F32)
    act = (gt * jax.nn.sigmoid(gt) * up).astype(BF16)
    lane = lax.broadcasted_iota(jnp.int32, c_ref.shape, 1)
    c_e = jnp.sum(jnp.where(lane == e, c_ref[...], 0.0), axis=-1, keepdims=True)
    acc_ref[...] += c_e * jnp.dot(act, wd_ref[0], preferred_element_type=F32)

    @pl.when((e == pl.num_programs(1) - 1) & (f == pl.num_programs(2) - 1))
    def _():
        o_ref[...] = x_ref[...] + acc_ref[...]


def ffn_moe(x, g, combine, w_gate_up, w_down, *, tm, tf):
    n, d = x.shape
    n_exp, d_ff = w_down.shape[0], w_down.shape[1]
    tm = min(tm, n)
    assert n % tm == 0 and d_ff % tf == 0 and tf % LANES == 0
    nf = d_ff // tf
    return pl.pallas_call(
        _moe_kernel,
        grid=(n // tm, n_exp, nf),
        in_specs=[
            pl.BlockSpec((tm, d), lambda i, e, f: (i, 0)),
            pl.BlockSpec((1, d), lambda i, e, f: (0, 0)),
            pl.BlockSpec((tm, LANES), lambda i, e, f: (i, 0)),
            pl.BlockSpec((1, d, tf), lambda i, e, f: (e, 0, f)),
            pl.BlockSpec((1, d, tf), lambda i, e, f: (e, 0, nf + f)),
            pl.BlockSpec((1, tf, d), lambda i, e, f: (e, f, 0)),
        ],
        out_specs=pl.BlockSpec((tm, d), lambda i, e, f: (i, 0)),
        out_shape=jax.ShapeDtypeStruct((n, d), F32),
        scratch_shapes=[pltpu.VMEM((tm, d), BF16), pltpu.VMEM((tm, d), F32)],
        compiler_params=_params("parallel", "arbitrary", "arbitrary"),
        name="ffn_moe",
    )(x, g.reshape(1, d), combine, w_gate_up, w_gate_up, w_down)


def _slab_weights(w_in):
    d = w_in.shape[0]
    sbw = SB_HEADS * SB_HEAD_DIM
    glv = GLA_HEADS * GLA_DV
    caw = CA_HEADS * CA_HEAD_DIM
    sizes = (sbw, sbw, sbw, GLA_KW, GLA_KW, glv, GLA_GATE_RANK, glv, caw, N_BRANCH * D_MODEL)
    parts, off = [], 0
    for s in sizes:
        parts.append(w_in[:, off:off + s])
        off += s
    sb_q, sb_k, sb_v, g_q, g_k, g_v, g_a, g_r, ca_q, gates = parts
    pad = lambda cols: jnp.zeros((d, cols), w_in.dtype)
    cols = [sb_q, sb_k, sb_v, g_v, g_r, ca_q, gates, g_q, g_k, g_a, pad(2 * LANES - GLA_GATE_RANK)]
    w = jnp.concatenate(cols, axis=1).astype(BF16)
    assert w.shape[1] == N_SLABS * LANES
    return w


def kernel(x, mem, mix_norm_g, w_in, sb_q_g, sb_k_g, gla_a_up, gla_a_b, gla_o_g, ca_q_g, ca_k_g, mem_norm_g, w_mem_kv, w_br_sb, w_br_gla, w_br_ca, w_out, ffn_norm_g, ff_w_gate_up, ff_w_down, moe_w_router, moe_w_gate_up, moe_w_down):
    batch, seq, d = x.shape
    depth = w_in.shape[0]
    n = batch * seq
    xf = x.reshape(n, d)
    memf = mem.reshape(-1, d)
    bf = lambda a: a.astype(BF16)

    for layer in range(depth):
        proj = norm_matmul(xf, mix_norm_g[layer], _slab_weights(w_in[layer]), tm=512, tn=2304, slabs=True)
        mkv = norm_matmul(memf, mem_norm_g[layer], bf(w_mem_kv[layer]), tm=512, tn=1024, slabs=False)
        o_a = sb_attention(proj, sb_q_g[layer], sb_k_g[layer], batch=batch, seq=seq, tq=256, tk=256)
        o_b = gla(proj, gla_a_up[layer], gla_a_b[layer], gla_o_g[layer], batch=batch, seq=seq)
        xf = merge_branches(xf, o_a, o_b, proj, mkv, ca_q_g[layer], ca_k_g[layer],
                            bf(w_br_sb[layer]), bf(w_br_gla[layer]), bf(w_br_ca[layer]), bf(w_out[layer]),
                            batch=batch, seq=seq, tm=512)
        j = layer // 2
        if layer % 2 == 0:
            xf = ffn_dense(xf, ffn_norm_g[layer], bf(ff_w_gate_up[j]), bf(ff_w_down[j]), tm=512, tf=1408)
        else:
            combine = router(xf, ffn_norm_g[layer], moe_w_router[j], tm=512)
            xf = ffn_moe(xf, ffn_norm_g[layer], combine, bf(moe_w_gate_up[j]), bf(moe_w_down[j]), tm=512, tf=896)
    return xf.reshape(batch, seq, d)
```

```python
import functools

import jax
import jax.numpy as jnp
from jax import lax
from jax.experimental import pallas as pl
from jax.experimental.pallas import tpu as pltpu

F32 = jnp.float32
BF16 = jnp.bfloat16

LANES = 128
EPS = 1e-6
LOG2E = 1.4426950408889634
LN2 = 0.6931471805599453
MASKED = 1e30
VMEM_LIMIT_BYTES = 56 * 1024 * 1024

D_MODEL = 1024
SB_HEADS, SB_HEAD_DIM = 8, 64
GLA_HEADS, GLA_DK, GLA_DV = 4, 64, 128
GLA_GATE_RANK, GLA_GATE_TAU, GLA_CHUNK = 16, 16.0, 64
CA_HEADS, CA_HEAD_DIM = 4, 128
N_BRANCH = 3
N_EXPERTS, TOP_K = 8, 2

SLAB_SB_Q, SLAB_SB_K, SLAB_SB_V = 0, 4, 8
SLAB_GLA_V, SLAB_GLA_R = 12, 16
SLAB_CA_Q = 20
SLAB_GATES = 24
SLAB_GLA_Q, SLAB_GLA_K, SLAB_GLA_A = 48, 50, 52
N_SLABS = 54


def _params(*semantics):
    return pltpu.CompilerParams(dimension_semantics=semantics, vmem_limit_bytes=VMEM_LIMIT_BYTES)


def _rms_rows(x, eps=EPS):
    return x * lax.rsqrt(jnp.mean(x * x, axis=-1, keepdims=True) + eps)


def _rms_half_lanes(x, eps=EPS):
    lane = lax.broadcasted_iota(jnp.int32, x.shape, 1)
    lo = lane < SB_HEAD_DIM
    sq = x * x
    s_lo = jnp.sum(jnp.where(lo, sq, 0.0), axis=-1, keepdims=True)
    s_hi = jnp.sum(jnp.where(lo, 0.0, sq), axis=-1, keepdims=True)
    ms = jnp.where(lo, s_lo, s_hi) * (1.0 / SB_HEAD_DIM)
    return x * lax.rsqrt(ms + eps)


def _split_bf16(x):
    hi = x.astype(BF16)
    lo = (x - hi.astype(F32)).astype(BF16)
    return hi, lo


def _norm_matmul_kernel(x_ref, g_ref, w_ref, o_ref, xn_ref, *, n_slab):
    @pl.when(pl.program_id(1) == 0)
    def _():
        xn_ref[...] = (_rms_rows(x_ref[...]) * g_ref[...]).astype(BF16)

    acc = jnp.dot(xn_ref[...], w_ref[...], preferred_element_type=F32)
    if n_slab is None:
        o_ref[...] = acc.astype(o_ref.dtype)
    else:
        for s in range(n_slab):
            o_ref[s] = acc[:, s * LANES:(s + 1) * LANES].astype(o_ref.dtype)


def norm_matmul(x, g, w, *, tm, tn, slabs):
    n, k = x.shape
    m = w.shape[1]
    tm = min(tm, n)
    tn = min(tn, m)
    assert n % tm == 0 and m % tn == 0 and tn % LANES == 0
    if slabs:
        n_slab = tn // LANES
        out_shape = jax.ShapeDtypeStruct((m // LANES, n, LANES), BF16)
        out_spec = pl.BlockSpec((n_slab, tm, LANES), lambda i, j: (j, i, 0))
    else:
        n_slab = None
        out_shape = jax.ShapeDtypeStruct((n, m), BF16)
        out_spec = pl.BlockSpec((tm, tn), lambda i, j: (i, j))
    return pl.pallas_call(
        functools.partial(_norm_matmul_kernel, n_slab=n_slab),
        grid=(n // tm, m // tn),
        in_specs=[
            pl.BlockSpec((tm, k), lambda i, j: (i, 0)),
            pl.BlockSpec((1, k), lambda i, j: (0, 0)),
            pl.BlockSpec((k, tn), lambda i, j: (0, j)),
        ],
        out_specs=out_spec,
        out_shape=out_shape,
        scratch_shapes=[pltpu.VMEM((tm, k), BF16)],
        compiler_params=_params("parallel", "arbitrary"),
        name="norm_matmul",
    )(x, g.reshape(1, k), w)


def _softplus(z):
    return jnp.maximum(z, 0.0) + jnp.log(1.0 + jnp.exp(-jnp.abs(z)))


def _sb_items(nq, n_diag):
    qi, kj = [], []
    for i in range(nq):
        for j in reversed(range((i + 1) * n_diag)):
            qi.append(i)
            kj.append(j)
    return qi, kj


def _sb_kernel(qi_tab, kj_tab, q_ref, k_ref, v_ref, qg_ref, kg_ref, o_ref,
               qh_ref, kn_ref, z_ref, hi_ref, lo_ref, bt_ref, w_ref, rs_ref, *, tq, tk, seq, n_items):
    n_diag = tq // tk
    rows = min(512, seq)
    lane = lax.broadcasted_iota(jnp.int32, (rows, LANES), 1)
    scale = SB_HEAD_DIM ** -0.5

    def norm_chunk(c, carry):
        r0 = pl.multiple_of(c * rows, rows)
        kk = k_ref[0, pl.ds(r0, rows), :].astype(F32)
        kn_ref[pl.ds(r0, rows), :] = (_rms_half_lanes(kk) * kg_ref[...]).astype(BF16)
        qn = _rms_half_lanes(q_ref[0, pl.ds(r0, rows), :].astype(F32)) * (qg_ref[...] * scale)
        qh_ref[0, pl.ds(r0, rows), :] = jnp.where(lane < SB_HEAD_DIM, qn, 0.0).astype(BF16)
        qh_ref[1, pl.ds(r0, rows), :] = jnp.where(lane >= SB_HEAD_DIM, qn, 0.0).astype(BF16)
        return carry
    lax.fori_loop(0, seq // rows, norm_chunk, 0)

    z_ref[...] = jnp.zeros_like(z_ref)
    hi_ref[...] = jnp.zeros_like(hi_ref)
    lo_ref[...] = jnp.zeros_like(lo_ref)
    bt_ref[...] = jnp.zeros_like(bt_ref)
    w_ref[...] = jnp.zeros_like(w_ref)
    rs_ref[...] = jnp.zeros_like(rs_ref)

    rj = lax.broadcasted_iota(jnp.int32, (tk, tk), 0)
    cs = lax.broadcasted_iota(jnp.int32, (tk, tk), 1)
    u_strict = jnp.where(rj > cs, -1.0, 0.0).astype(BF16)
    u2 = jnp.concatenate([u_strict, u_strict], axis=0)
    rel = (lax.broadcasted_iota(jnp.int32, (tq, tk), 1) - lax.broadcasted_iota(jnp.int32, (tq, tk), 0))
    out_lane = lax.broadcasted_iota(jnp.int32, (tq, LANES), 1) < SB_HEAD_DIM

    def item(m):
        m = jnp.clip(m, 0, n_items - 1)
        return qi_tab[m], kj_tab[m]

    def step(n, slot, carries, accs):
        prev = 1 - slot
        qi1, kj1 = item(n)
        kb = kn_ref[pl.ds(pl.multiple_of(kj1 * tk, tk), tk), :]
        for h in range(2):
            qh = qh_ref[h, pl.ds(pl.multiple_of(qi1 * tq, tq), tq), :]
            z_ref[slot, h] = lax.dot_general(qh, kb, (((1,), (1,)), ((), ())), preferred_element_type=F32)
        qi2, kj2 = item(n - 1)
        keep = rel < (qi2 * tq - kj2 * tk)
        for h in range(2):
            z = jnp.where(keep, z_ref[prev, h], -MASKED)
            e = jnp.exp2(jnp.abs(z) * (-LOG2E))
            r = 1.0 / (1.0 + e)
            sp = jnp.maximum(z, 0.0) - jnp.log(r)
            beta = jnp.where(z >= 0.0, r, e * r)
            hi, lo = _split_bf16(sp)
            hi_ref[slot, h] = hi
            lo_ref[slot, h] = lo
            bt_ref[slot, h] = beta.astype(BF16)
            rs_ref[slot, h] = jnp.broadcast_to(jnp.sum(sp, axis=1, keepdims=True), (tq, LANES))
        qi3, kj3 = item(n - 2)
        first3 = kj3 == (qi3 + 1) * n_diag - 1
        new_carries = []
        for h in range(2):
            hl = jnp.concatenate([hi_ref[prev, h], lo_ref[prev, h]], axis=1)
            t = jnp.dot(hl, u2, preferred_element_type=F32)
            c_in = jnp.where(first3, 0.0, carries[h])
            c_wide = jnp.concatenate([c_in] * (tk // LANES), axis=1)
            w = jnp.exp2((t + c_wide) * LOG2E) * bt_ref[prev, h].astype(F32)
            w_ref[slot, h] = w.astype(BF16)
            new_carries.append(c_in - rs_ref[prev, h])
        qi4, kj4 = item(n - 3)
        first4 = kj4 == (qi4 + 1) * n_diag - 1
        valid4 = (n - 3) < n_items
        vb = v_ref[0, pl.ds(pl.multiple_of(kj4 * tk, tk), tk), :]
        new_accs = []
        for h in range(2):
            pv = jnp.dot(w_ref[prev, h], vb, preferred_element_type=F32)
            new_accs.append(jnp.where(first4, 0.0, accs[h]) + jnp.where(valid4, pv, 0.0))
        o_ref[0, pl.ds(pl.multiple_of(qi4 * tq, tq), tq), :] = (
            jnp.where(out_lane, new_accs[0], new_accs[1]).astype(o_ref.dtype))
        return new_carries, new_accs

    def body(n2, st):
        carries, accs = [st[0], st[1]], [st[2], st[3]]
        carries, accs = step(2 * n2, 0, carries, accs)
        carries, accs = step(2 * n2 + 1, 1, carries, accs)
        return (carries[0], carries[1], accs[0], accs[1])

    zc = jnp.zeros((tq, LANES), F32)
    za = jnp.zeros((tq, LANES), F32)
    n_steps = n_items + 3
    lax.fori_loop(0, (n_steps + 1) // 2, body, (zc, zc, za, za))


def sb_attention(proj, q_gain, k_gain, *, batch, seq, tq, tk):
    tq = min(tq, seq)
    tk = min(tk, tq)
    assert seq % tq == 0 and tq % tk == 0
    n_pairs = SB_HEADS // 2
    qi, kj = _sb_items(seq // tq, tq // tk)
    g2 = lambda g: jnp.concatenate([g, g]).reshape(1, LANES).astype(F32)
    seq_blk = lambda s0: pl.BlockSpec((1, seq, LANES), lambda b, p, qt, kt: (s0 + p, b, 0))
    stage = lambda dt: pltpu.VMEM((2, 2, tq, tk), dt)
    return pl.pallas_call(
        functools.partial(_sb_kernel, tq=tq, tk=tk, seq=seq, n_items=len(qi)),
        grid_spec=pltpu.PrefetchScalarGridSpec(
            num_scalar_prefetch=2,
            grid=(batch, n_pairs),
            in_specs=[
                seq_blk(SLAB_SB_Q), seq_blk(SLAB_SB_K), seq_blk(SLAB_SB_V),
                pl.BlockSpec((1, LANES), lambda b, p, qt, kt: (0, 0)),
                pl.BlockSpec((1, LANES), lambda b, p, qt, kt: (0, 0)),
            ],
            out_specs=pl.BlockSpec((1, seq, LANES), lambda b, p, qt, kt: (p, b, 0)),
            scratch_shapes=[
                pltpu.VMEM((2, seq, LANES), BF16),
                pltpu.VMEM((seq, LANES), BF16),
                stage(F32), stage(BF16), stage(BF16), stage(BF16), stage(BF16),
                pltpu.VMEM((2, 2, tq, LANES), F32),
            ],
        ),
        out_shape=jax.ShapeDtypeStruct((n_pairs, batch * seq, LANES), BF16),
        compiler_params=_params("parallel", "parallel"),
        name="sb_attention",
    )(jnp.asarray(qi, jnp.int32), jnp.asarray(kj, jnp.int32), proj, proj, proj, g2(q_gain), g2(k_gain))


GLA_ROWS = 256
GLA_KW = GLA_HEADS * GLA_DK


def _gla_kernel(q_ref, k_ref, v_ref, a_ref, r_ref, aup_ref, ab_ref, og_ref, o_ref, state_ref):
    rows = GLA_ROWS
    n_chunks = rows // GLA_CHUNK

    @pl.when(pl.program_id(1) == 0)
    def _():
        state_ref[...] = jnp.zeros_like(state_ref)

    pre = jnp.dot(a_ref[0], aup_ref[...], preferred_element_type=F32) + ab_ref[...]
    log_a = -_softplus(-pre) * (1.0 / GLA_GATE_TAU)
    la_hi, la_lo = _split_bf16(log_a)

    ri = lax.broadcasted_iota(jnp.int32, (rows, rows), 0)
    ci = lax.broadcasted_iota(jnp.int32, (rows, rows), 1)
    same_chunk = (ri // GLA_CHUNK) == (ci // GLA_CHUNK)
    tri = same_chunk & (ri >= ci)
    tri_b = jnp.where(tri, 1.0, 0.0).astype(BF16)
    ones_b = jnp.where(same_chunk, 1.0, 0.0).astype(BF16)
    cum = (jnp.dot(tri_b, la_hi, preferred_element_type=F32)
           + jnp.dot(tri_b, la_lo, preferred_element_type=F32))
    cum_last = (jnp.dot(ones_b, la_hi, preferred_element_type=F32)
                + jnp.dot(ones_b, la_lo, preferred_element_type=F32))

    q_all = jnp.concatenate([q_ref[0], q_ref[1]], axis=-1).astype(F32)
    k_all = jnp.concatenate([k_ref[0], k_ref[1]], axis=-1).astype(F32)
    q_dec = q_all * (GLA_DK ** -0.5) * jnp.exp(cum)
    k_inv = (k_all * jnp.exp(-cum)).astype(BF16)
    k_end = (k_all * jnp.exp(cum_last - cum)).astype(BF16)
    v_all = jnp.concatenate([v_ref[h] for h in range(GLA_HEADS)], axis=-1)

    lane = lax.broadcasted_iota(jnp.int32, (rows, GLA_KW), 1)
    q_heads = [jnp.where((lane // GLA_DK) == h, q_dec, 0.0).astype(BF16) for h in range(GLA_HEADS)]

    o_heads = []
    for h in range(GLA_HEADS):
        sc = lax.dot_general(q_heads[h], k_inv, (((1,), (1,)), ((), ())), preferred_element_type=F32)
        sc = jnp.where(tri, sc, 0.0).astype(BF16)
        o_heads.append(jnp.dot(sc, v_ref[h], preferred_element_type=F32))

    ones_cols = jnp.ones((GLA_CHUNK, LANES), BF16)
    state = state_ref[...]
    inter = [[] for _ in range(GLA_HEADS)]
    for c in range(n_chunks):
        r0, r1 = c * GLA_CHUNK, (c + 1) * GLA_CHUNK
        state_b = state.astype(BF16)
        for h in range(GLA_HEADS):
            inter[h].append(jnp.dot(q_heads[h][r0:r1], state_b, preferred_element_type=F32))
        kv = lax.dot_general(k_end[r0:r1], v_all[r0:r1], (((0,), (0,)), ((), ())),
                             preferred_element_type=F32)
        kv_sel = jnp.concatenate(
            [kv[h * GLA_DK:(h + 1) * GLA_DK, h * GLA_DV:(h + 1) * GLA_DV] for h in range(GLA_HEADS)], axis=0)
        tot = (lax.dot_general(la_hi[r0:r1], ones_cols, (((0,), (0,)), ((), ())), preferred_element_type=F32)
               + lax.dot_general(la_lo[r0:r1], ones_cols, (((0,), (0,)), ((), ())), preferred_element_type=F32))
        state = jnp.exp(tot) * state + kv_sel
    state_ref[...] = state

    for h in range(GLA_HEADS):
        o = o_heads[h] + jnp.concatenate(inter[h], axis=0)
        o = _rms_rows(o) * og_ref[...]
        r = r_ref[h].astype(F32)
        o_ref[h] = (o * (r * jax.nn.sigmoid(r))).astype(o_ref.dtype)


def gla(proj, a_up, a_b, o_gain, *, batch, seq):
    rows = GLA_ROWS
    assert seq % rows == 0
    nb = seq // rows
    a_up_p = jnp.zeros((LANES, GLA_KW), BF16).at[:GLA_GATE_RANK].set(a_up.astype(BF16))
    row_blk = lambda n, s0: pl.BlockSpec((n, rows, LANES), lambda b, i: (s0 // n, b * nb + i, 0))
    return pl.pallas_call(
        _gla_kernel,
        grid=(batch, nb),
        in_specs=[
            row_blk(2, SLAB_GLA_Q), row_blk(2, SLAB_GLA_K), row_blk(4, SLAB_GLA_V),
            row_blk(1, SLAB_GLA_A), row_blk(4, SLAB_GLA_R),
            pl.BlockSpec((LANES, GLA_KW), lambda b, i: (0, 0)),
            pl.BlockSpec((1, GLA_KW), lambda b, i: (0, 0)),
            pl.BlockSpec((1, GLA_DV), lambda b, i: (0, 0)),
        ],
        out_specs=pl.BlockSpec((GLA_HEADS, rows, LANES), lambda b, i: (0, b * nb + i, 0)),
        out_shape=jax.ShapeDtypeStruct((GLA_HEADS, batch * seq, LANES), BF16),
        scratch_shapes=[pltpu.VMEM((GLA_KW, GLA_DV), F32)],
        compiler_params=_params("parallel", "arbitrary"),
        name="gla",
    )(proj, proj, proj, proj, proj, a_up_p,
      a_b.reshape(1, GLA_KW).astype(F32), o_gain.reshape(1, GLA_DV).astype(F32))


def _merge_kernel(x_ref, oa_ref, ob_ref, cq_ref, gate_ref, mkv_ref, cqg_ref, ckg_ref,
                  wsb_ref, wgla_ref, wca_ref, wout_ref, o_ref):
    ca_w = CA_HEADS * CA_HEAD_DIM
    oc = []
    for h in range(CA_HEADS):
        q = _rms_rows(cq_ref[h].astype(F32)) * (cqg_ref[...] * CA_HEAD_DIM ** -0.5)
        kk = _rms_rows(mkv_ref[:, h * CA_HEAD_DIM:(h + 1) * CA_HEAD_DIM].astype(F32)) * ckg_ref[...]
        s = lax.dot_general(q.astype(BF16), kk.astype(BF16), (((1,), (1,)), ((), ())),
                            preferred_element_type=F32)
        e = jnp.exp(s - jnp.max(s, axis=-1, keepdims=True))
        p = e * (1.0 / jnp.sum(e, axis=-1, keepdims=True))
        vv = mkv_ref[:, ca_w + h * CA_HEAD_DIM: ca_w + (h + 1) * CA_HEAD_DIM]
        oc.append(jnp.dot(p.astype(BF16), vv, preferred_element_type=F32).astype(BF16))
    o_c = jnp.concatenate(oc, axis=-1)
    o_a = jnp.concatenate([oa_ref[s] for s in range(4)], axis=-1)
    o_b = jnp.concatenate([ob_ref[s] for s in range(4)], axis=-1)

    def gate(b):
        g = jnp.concatenate([gate_ref[b * 8 + s] for s in range(8)], axis=-1).astype(F32)
        return jax.nn.sigmoid(g)

    merged = gate(0) * jnp.dot(o_a, wsb_ref[...], preferred_element_type=F32)
    merged += gate(1) * jnp.dot(o_b, wgla_ref[...], preferred_element_type=F32)
    merged += gate(2) * jnp.dot(o_c, wca_ref[...], preferred_element_type=F32)
    o_ref[...] = x_ref[...] + jnp.dot(merged.astype(BF16), wout_ref[...], preferred_element_type=F32)


def merge_branches(x, o_a, o_b, proj, mkv, cq_gain, ck_gain, w_sb, w_gla, w_ca, w_out, *, batch, seq, tm):
    n, d = x.shape
    tm = min(tm, seq)
    assert seq % tm == 0
    nt = seq // tm
    mem_len = mkv.shape[0] // batch
    full = lambda a: pl.BlockSpec(a.shape, lambda i: (0,) * a.ndim)
    cqg = cq_gain.reshape(1, CA_HEAD_DIM).astype(F32)
    ckg = ck_gain.reshape(1, CA_HEAD_DIM).astype(F32)
    return pl.pallas_call(
        _merge_kernel,
        grid=(n // tm,),
        in_specs=[
            pl.BlockSpec((tm, d), lambda i: (i, 0)),
            pl.BlockSpec((4, tm, LANES), lambda i: (0, i, 0)),
            pl.BlockSpec((4, tm, LANES), lambda i: (0, i, 0)),
            pl.BlockSpec((4, tm, LANES), lambda i: (SLAB_CA_Q // 4, i, 0)),
            pl.BlockSpec((24, tm, LANES), lambda i: (SLAB_GATES // 24, i, 0)),
            pl.BlockSpec((mem_len, mkv.shape[1]), lambda i: (i // nt, 0)),
            full(cqg), full(ckg), full(w_sb), full(w_gla), full(w_ca), full(w_out),
        ],
        out_specs=pl.BlockSpec((tm, d), lambda i: (i, 0)),
        out_shape=jax.ShapeDtypeStruct((n, d), F32),
        compiler_params=_params("parallel"),
        name="merge_branches",
    )(x, o_a, o_b, proj, proj, mkv,
      cqg, ckg, w_sb, w_gla, w_ca, w_out)


def _ffn_kernel(x_ref, g_ref, wg_ref, wu_ref, wd_ref, o_ref, hn_ref, acc_ref):
    f = pl.program_id(1)

    @pl.when(f == 0)
    def _():
        hn_ref[...] = (_rms_rows(x_ref[...]) * g_ref[...]).astype(BF16)
        acc_ref[...] = jnp.zeros_like(acc_ref)

    h = hn_ref[...]
    gt = jnp.dot(h, wg_ref[...], preferred_element_type=F32)
    up = jnp.dot(h, wu_ref[...], preferred_element_type=F32)
    act = (gt * jax.nn.sigmoid(gt) * up).astype(BF16)
    acc_ref[...] += jnp.dot(act, wd_ref[...], preferred_element_type=F32)

    @pl.when(f == pl.num_programs(1) - 1)
    def _():
        o_ref[...] = x_ref[...] + acc_ref[...]


def ffn_dense(x, g, w_gate_up, w_down, *, tm, tf):
    n, d = x.shape
    d_ff = w_down.shape[0]
    tm = min(tm, n)
    assert n % tm == 0 and d_ff % tf == 0 and tf % LANES == 0
    nf = d_ff // tf
    return pl.pallas_call(
        _ffn_kernel,
        grid=(n // tm, nf),
        in_specs=[
            pl.BlockSpec((tm, d), lambda i, f: (i, 0)),
            pl.BlockSpec((1, d), lambda i, f: (0, 0)),
            pl.BlockSpec((d, tf), lambda i, f: (0, f)),
            pl.BlockSpec((d, tf), lambda i, f: (0, nf + f)),
            pl.BlockSpec((tf, d), lambda i, f: (f, 0)),
        ],
        out_specs=pl.BlockSpec((tm, d), lambda i, f: (i, 0)),
        out_shape=jax.ShapeDtypeStruct((n, d), F32),
        scratch_shapes=[pltpu.VMEM((tm, d), BF16), pltpu.VMEM((tm, d), F32)],
        compiler_params=_params("parallel", "arbitrary"),
        name="ffn_dense",
    )(x, g.reshape(1, d), w_gate_up, w_gate_up, w_down)


def _router_kernel(x_ref, g_ref, wr_ref, c_ref):
    hn = _rms_rows(x_ref[...]) * g_ref[...]
    logits = jnp.dot(hn, wr_ref[...], preferred_element_type=F32, precision=lax.Precision.HIGHEST)
    lane = lax.broadcasted_iota(jnp.int32, logits.shape, 1).astype(F32)
    neg = jnp.float32(-jnp.inf)
    logits = jnp.where(lane < N_EXPERTS, logits, neg)
    m1 = jnp.max(logits, axis=-1, keepdims=True)
    i1 = jnp.min(jnp.where(logits == m1, lane, float(LANES)), axis=-1, keepdims=True)
    rest = jnp.where(lane == i1, neg, logits)
    m2 = jnp.max(rest, axis=-1, keepdims=True)
    i2 = jnp.min(jnp.where(rest == m2, lane, float(LANES)), axis=-1, keepdims=True)
    e2 = jnp.exp(m2 - m1)
    inv = 1.0 / (1.0 + e2)
    c_ref[...] = (jnp.where(lane == 0.0, inv, 0.0) + jnp.where(lane == 1.0, e2 * inv, 0.0)
                  + jnp.where(lane == 2.0, i1, 0.0) + jnp.where(lane == 3.0, i2, 0.0))


def router(x, g, w_router, *, tm):
    n, d = x.shape
    tm = min(tm, n)
    wr = jnp.zeros((d, LANES), F32).at[:, :N_EXPERTS].set(w_router)
    return pl.pallas_call(
        _router_kernel,
        grid=(n // tm,),
        in_specs=[
            pl.BlockSpec((tm, d), lambda i: (i, 0)),
            pl.BlockSpec((1, d), lambda i: (0, 0)),
            pl.BlockSpec((d, LANES), lambda i: (0, 0)),
        ],
        out_specs=pl.BlockSpec((tm, LANES), lambda i: (i, 0)),
        out_shape=jax.ShapeDtypeStruct((n, LANES), F32),
        compiler_params=_params("parallel"),
        name="router",
    )(x, g.reshape(1, d), wr)


def _row_copy(src_hbm, row, dst_vmem, r, sem):
    return pltpu.make_async_copy(src_hbm.at[pl.ds(row, 1)], dst_vmem.at[pl.ds(r, 1)], sem)


def _gather_kernel(idx_ref, src_hbm, o_ref, sem):
    rows = o_ref.shape[0]

    def issue(r, carry):
        _row_copy(src_hbm, idx_ref[0, 0, r], o_ref, r, sem).start()
        return carry
    lax.fori_loop(0, rows, issue, 0)

    def drain(r, carry):
        _row_copy(src_hbm, 0, o_ref, r, sem).wait()
        return carry
    lax.fori_loop(0, rows, drain, 0)


def gather_rows(src, idx, *, tg):
    rows = idx.shape[0]
    d = src.shape[1]
    assert rows % tg == 0
    return pl.pallas_call(
        _gather_kernel,
        grid=(rows // tg,),
        in_specs=[
            pl.BlockSpec((1, 1, tg), lambda i: (i, 0, 0), memory_space=pltpu.SMEM),
            pl.BlockSpec(memory_space=pl.ANY),
        ],
        out_specs=pl.BlockSpec((tg, d), lambda i: (i, 0)),
        out_shape=jax.ShapeDtypeStruct((rows, d), src.dtype),
        scratch_shapes=[pltpu.SemaphoreType.DMA],
        compiler_params=_params("arbitrary"),
        name="gather_rows",
    )(idx.reshape(rows // tg, 1, tg), src)


def _expert_ffn_kernel(te_ref, nu_ref, x_ref, g_ref, wg_ref, wu_ref, wd_ref, o_ref, hn_ref, acc_ref):
    i = pl.program_id(0)
    f = pl.program_id(1)
    used = i < nu_ref[0]

    @pl.when(used & (f == 0))
    def _():
        hn_ref[...] = (_rms_rows(x_ref[...]) * g_ref[...]).astype(BF16)
        acc_ref[...] = jnp.zeros_like(acc_ref)

    @pl.when(used)
    def _():
        h = hn_ref[...]
        gt = jnp.dot(h, wg_ref[0], preferred_element_type=F32)
        up = jnp.dot(h, wu_ref[0], preferred_element_type=F32)
        act = (gt * jax.nn.sigmoid(gt) * up).astype(BF16)
        acc_ref[...] += jnp.dot(act, wd_ref[0], preferred_element_type=F32)

    @pl.when(f == pl.num_programs(1) - 1)
    def _():
        o_ref[...] = jnp.where(used, acc_ref[...], 0.0)


def expert_ffn(xs, g, tile_expert, n_used, w_gate_up, w_down, *, tm, tf):
    rows, d = xs.shape
    d_ff = w_down.shape[1]
    assert rows % tm == 0 and d_ff % tf == 0 and tf % LANES == 0
    nf = d_ff // tf
    return pl.pallas_call(
        _expert_ffn_kernel,
        grid_spec=pltpu.PrefetchScalarGridSpec(
            num_scalar_prefetch=2,
            grid=(rows // tm, nf),
            in_specs=[
                pl.BlockSpec((tm, d), lambda i, f, te, nu: (i, 0)),
                pl.BlockSpec((1, d), lambda i, f, te, nu: (0, 0)),
                pl.BlockSpec((1, d, tf), lambda i, f, te, nu: (te[i], 0, f)),
                pl.BlockSpec((1, d, tf), lambda i, f, te, nu: (te[i], 0, nf + f)),
                pl.BlockSpec((1, tf, d), lambda i, f, te, nu: (te[i], f, 0)),
            ],
            out_specs=pl.BlockSpec((tm, d), lambda i, f, te, nu: (i, 0)),
            scratch_shapes=[pltpu.VMEM((tm, d), BF16), pltpu.VMEM((tm, d), F32)],
        ),
        out_shape=jax.ShapeDtypeStruct((rows, d), F32),
        compiler_params=_params("arbitrary", "arbitrary"),
        name="expert_ffn",
    )(tile_expert, n_used, xs, g.reshape(1, d), w_gate_up, w_gate_up, w_down)


def _combine_kernel(p1_ref, p2_ref, x_ref, rw_ref, ys_hbm, o_ref, a_ref, b_ref, sem):
    rows = o_ref.shape[0]

    def issue(r, carry):
        _row_copy(ys_hbm, p1_ref[0, 0, r], a_ref, r, sem.at[0]).start()
        _row_copy(ys_hbm, p2_ref[0, 0, r], b_ref, r, sem.at[1]).start()
        return carry
    lax.fori_loop(0, rows, issue, 0)

    def drain(r, carry):
        _row_copy(ys_hbm, 0, a_ref, r, sem.at[0]).wait()
        _row_copy(ys_hbm, 0, b_ref, r, sem.at[1]).wait()
        return carry
    lax.fori_loop(0, rows, drain, 0)

    rw = rw_ref[...]
    o_ref[...] = x_ref[...] + (rw[:, 0:1] * a_ref[...] + rw[:, 1:2] * b_ref[...])


def moe_combine(x, route, ys, pos1, pos2, *, tc):
    n, d = x.shape
    assert n % tc == 0
    idx_spec = pl.BlockSpec((1, 1, tc), lambda i: (i, 0, 0), memory_space=pltpu.SMEM)
    return pl.pallas_call(
        _combine_kernel,
        grid=(n // tc,),
        in_specs=[
            idx_spec, idx_spec,
            pl.BlockSpec((tc, d), lambda i: (i, 0)),
            pl.BlockSpec((tc, LANES), lambda i: (i, 0)),
            pl.BlockSpec(memory_space=pl.ANY),
        ],
        out_specs=pl.BlockSpec((tc, d), lambda i: (i, 0)),
        out_shape=jax.ShapeDtypeStruct((n, d), F32),
        scratch_shapes=[pltpu.VMEM((tc, d), F32), pltpu.VMEM((tc, d), F32), pltpu.SemaphoreType.DMA((2,))],
        compiler_params=_params("arbitrary"),
        name="moe_combine",
    )(pos1.reshape(n // tc, 1, tc), pos2.reshape(n // tc, 1, tc), x, route, ys)


def _dispatch_plan(route, *, tm):
    n = route.shape[0]
    ids = route[:, 2:4].astype(jnp.int32)
    e_flat = ids.T.reshape(-1)
    onehot = (e_flat[:, None] == jnp.arange(N_EXPERTS)[None, :]).astype(jnp.int32)
    running = jnp.cumsum(onehot, axis=0)
    rank = jnp.sum(running * onehot, axis=1) - 1
    counts = running[-1]
    padded = ((counts + tm - 1) // tm) * tm
    ends = jnp.cumsum(padded)
    starts = ends - padded
    pos = starts[e_flat] + rank
    rows = TOP_K * n + N_EXPERTS * tm
    tok = jnp.tile(jnp.arange(n, dtype=jnp.int32), TOP_K)
    src = jnp.zeros((rows,), jnp.int32).at[pos].set(tok)
    tile_start = jnp.arange(rows // tm, dtype=jnp.int32) * tm
    tile_expert = jnp.minimum(jnp.sum(tile_start[:, None] >= ends[None, :], axis=1), N_EXPERTS - 1)
    n_used = (ends[-1] // tm).reshape(1).astype(jnp.int32)
    return src, tile_expert.astype(jnp.int32), n_used, pos[:n].astype(jnp.int32), pos[n:].astype(jnp.int32)


def ffn_moe(x, g, route, w_gate_up, w_down, *, tm, tf):
    src, tile_expert, n_used, pos1, pos2 = _dispatch_plan(route, tm=tm)
    xs = gather_rows(x, src, tg=tm)
    ys = expert_ffn(xs, g, tile_expert, n_used, w_gate_up, w_down, tm=tm, tf=tf)
    return moe_combine(x, route, ys, pos1, pos2, tc=min(256, x.shape[0]))


def _slab_weights(w_in):
    d = w_in.shape[0]
    sbw = SB_HEADS * SB_HEAD_DIM
    glv = GLA_HEADS * GLA_DV
    caw = CA_HEADS * CA_HEAD_DIM
    sizes = (sbw, sbw, sbw, GLA_KW, GLA_KW, glv, GLA_GATE_RANK, glv, caw, N_BRANCH * D_MODEL)
    parts, off = [], 0
    for s in sizes:
        parts.append(w_in[:, off:off + s])
        off += s
    sb_q, sb_k, sb_v, g_q, g_k, g_v, g_a, g_r, ca_q, gates = parts
    pad = lambda cols: jnp.zeros((d, cols), w_in.dtype)
    cols = [sb_q, sb_k, sb_v, g_v, g_r, ca_q, gates, g_q, g_k, g_a, pad(2 * LANES - GLA_GATE_RANK)]
    w = jnp.concatenate(cols, axis=1).astype(BF16)
    assert w.shape[1] == N_SLABS * LANES
    return w


def kernel(x, mem, mix_norm_g, w_in, sb_q_g, sb_k_g, gla_a_up, gla_a_b, gla_o_g, ca_q_g, ca_k_g, mem_norm_g, w_mem_kv, w_br_sb, w_br_gla, w_br_ca, w_out, ffn_norm_g, ff_w_gate_up, ff_w_down, moe_w_router, moe_w_gate_up, moe_w_down):
    batch, seq, d = x.shape
    depth = w_in.shape[0]
    n = batch * seq
    xf = x.reshape(n, d)
    memf = mem.reshape(-1, d)
    bf = lambda a: a.astype(BF16)

    for layer in range(depth):
        proj = norm_matmul(xf, mix_norm_g[layer], _slab_weights(w_in[layer]), tm=512, tn=2304, slabs=True)
        mkv = norm_matmul(memf, mem_norm_g[layer], bf(w_mem_kv[layer]), tm=512, tn=1024, slabs=False)
        o_a = sb_attention(proj, sb_q_g[layer], sb_k_g[layer], batch=batch, seq=seq, tq=256, tk=256)
        o_b = gla(proj, gla_a_up[layer], gla_a_b[layer], gla_o_g[layer], batch=batch, seq=seq)
        xf = merge_branches(xf, o_a, o_b, proj, mkv, ca_q_g[layer], ca_k_g[layer],
                            bf(w_br_sb[layer]), bf(w_br_gla[layer]), bf(w_br_ca[layer]), bf(w_out[layer]),
                            batch=batch, seq=seq, tm=512)
        j = layer // 2
        if layer % 2 == 0:
            xf = ffn_dense(xf, ffn_norm_g[layer], bf(ff_w_gate_up[j]), bf(ff_w_down[j]), tm=512, tf=1408)
        else:
            route = router(xf, ffn_norm_g[layer], moe_w_router[j], tm=512)
            xf = ffn_moe(xf, ffn_norm_g[layer], route, bf(moe_w_gate_up[j]), bf(moe_w_down[j]), tm=512, tf=896)
    return xf.reshape(batch, seq, d)
```

```python
import functools

import jax
import jax.numpy as jnp
from jax import lax
from jax.experimental import pallas as pl
from jax.experimental.pallas import tpu as pltpu

F32 = jnp.float32
BF16 = jnp.bfloat16

LANES = 128
EPS = 1e-6
LOG2E = 1.4426950408889634
LN2 = 0.6931471805599453
MASKED = 1e30
GATHER_UNROLL = 8
VMEM_LIMIT_BYTES = 56 * 1024 * 1024

D_MODEL = 1024
SB_HEADS, SB_HEAD_DIM = 8, 64
GLA_HEADS, GLA_DK, GLA_DV = 4, 64, 128
GLA_GATE_RANK, GLA_GATE_TAU, GLA_CHUNK = 16, 16.0, 64
CA_HEADS, CA_HEAD_DIM = 4, 128
N_BRANCH = 3
N_EXPERTS, TOP_K = 8, 2

SLAB_SB_Q, SLAB_SB_K, SLAB_SB_V = 0, 4, 8
SLAB_GLA_V, SLAB_GLA_R = 12, 16
SLAB_CA_Q = 20
SLAB_GATES = 24
SLAB_GLA_Q, SLAB_GLA_K, SLAB_GLA_A = 48, 50, 52
N_SLABS = 54


def _params(*semantics):
    return pltpu.CompilerParams(dimension_semantics=semantics, vmem_limit_bytes=VMEM_LIMIT_BYTES)


def _rms_rows(x, eps=EPS):
    return x * lax.rsqrt(jnp.mean(x * x, axis=-1, keepdims=True) + eps)


def _rms_half_lanes(x, eps=EPS):
    lane = lax.broadcasted_iota(jnp.int32, x.shape, 1)
    lo = lane < SB_HEAD_DIM
    sq = x * x
    s_lo = jnp.sum(jnp.where(lo, sq, 0.0), axis=-1, keepdims=True)
    s_hi = jnp.sum(jnp.where(lo, 0.0, sq), axis=-1, keepdims=True)
    ms = jnp.where(lo, s_lo, s_hi) * (1.0 / SB_HEAD_DIM)
    return x * lax.rsqrt(ms + eps)


def _split_bf16(x):
    hi = x.astype(BF16)
    lo = (x - hi.astype(F32)).astype(BF16)
    return hi, lo


def _norm_matmul_kernel(x_ref, g_ref, w_ref, o_ref, xn_ref, *, n_slab):
    @pl.when(pl.program_id(1) == 0)
    def _():
        xn_ref[...] = (_rms_rows(x_ref[...]) * g_ref[...]).astype(BF16)

    acc = jnp.dot(xn_ref[...], w_ref[...], preferred_element_type=F32)
    if n_slab is None:
        o_ref[...] = acc.astype(o_ref.dtype)
    else:
        for s in range(n_slab):
            o_ref[s] = acc[:, s * LANES:(s + 1) * LANES].astype(o_ref.dtype)


def norm_matmul(x, g, w, *, tm, tn, slabs):
    n, k = x.shape
    m = w.shape[1]
    tm = min(tm, n)
    tn = min(tn, m)
    assert n % tm == 0 and m % tn == 0 and tn % LANES == 0
    if slabs:
        n_slab = tn // LANES
        out_shape = jax.ShapeDtypeStruct((m // LANES, n, LANES), BF16)
        out_spec = pl.BlockSpec((n_slab, tm, LANES), lambda i, j: (j, i, 0))
    else:
        n_slab = None
        out_shape = jax.ShapeDtypeStruct((n, m), BF16)
        out_spec = pl.BlockSpec((tm, tn), lambda i, j: (i, j))
    return pl.pallas_call(
        functools.partial(_norm_matmul_kernel, n_slab=n_slab),
        grid=(n // tm, m // tn),
        in_specs=[
            pl.BlockSpec((tm, k), lambda i, j: (i, 0)),
            pl.BlockSpec((1, k), lambda i, j: (0, 0)),
            pl.BlockSpec((k, tn), lambda i, j: (0, j)),
        ],
        out_specs=out_spec,
        out_shape=out_shape,
        scratch_shapes=[pltpu.VMEM((tm, k), BF16)],
        compiler_params=_params("parallel", "arbitrary"),
        name="norm_matmul",
    )(x, g.reshape(1, k), w)


def _softplus(z):
    return jnp.maximum(z, 0.0) + jnp.log(1.0 + jnp.exp(-jnp.abs(z)))


def _sb_items(nq, n_diag):
    qi, kj = [], []
    for i in range(nq):
        for j in reversed(range((i + 1) * n_diag)):
            qi.append(i)
            kj.append(j)
    if len(qi) % 2 == 0:
        qi += [0] * n_diag
        kj += list(reversed(range(n_diag)))
    assert len(qi) % 2 == 1
    return qi, kj


def _sb_kernel(qi_tab, kj_tab, q_ref, k_ref, v_ref, qg_ref, kg_ref, o_ref,
               qh_ref, kn_ref, z_ref, sp_ref, bt_ref, w_ref, rs_ref, *, tq, tk, seq, n_items):
    n_diag = tq // tk
    rows = min(512, seq)
    lane = lax.broadcasted_iota(jnp.int32, (rows, LANES), 1)
    scale = SB_HEAD_DIM ** -0.5

    def norm_chunk(c, carry):
        r0 = pl.multiple_of(c * rows, rows)
        kk = k_ref[0, pl.ds(r0, rows), :].astype(F32)
        kn_ref[pl.ds(r0, rows), :] = (_rms_half_lanes(kk) * kg_ref[...]).astype(BF16)
        qn = _rms_half_lanes(q_ref[0, pl.ds(r0, rows), :].astype(F32)) * (qg_ref[...] * scale)
        qh_ref[0, pl.ds(r0, rows), :] = jnp.where(lane < SB_HEAD_DIM, qn, 0.0).astype(BF16)
        qh_ref[1, pl.ds(r0, rows), :] = jnp.where(lane >= SB_HEAD_DIM, qn, 0.0).astype(BF16)
        return carry
    lax.fori_loop(0, seq // rows, norm_chunk, 0)

    z_ref[...] = jnp.zeros_like(z_ref)
    sp_ref[...] = jnp.zeros_like(sp_ref)
    bt_ref[...] = jnp.zeros_like(bt_ref)
    w_ref[...] = jnp.zeros_like(w_ref)
    rs_ref[...] = jnp.zeros_like(rs_ref)

    rj = lax.broadcasted_iota(jnp.int32, (tk, tk), 0)
    cs = lax.broadcasted_iota(jnp.int32, (tk, tk), 1)
    u_strict = jnp.where(rj > cs, -1.0, 0.0).astype(BF16)
    rel = (lax.broadcasted_iota(jnp.int32, (tq, tk), 1) - lax.broadcasted_iota(jnp.int32, (tq, tk), 0))
    out_lane = lax.broadcasted_iota(jnp.int32, (tq, LANES), 1) < SB_HEAD_DIM

    def item(m):
        m = jnp.clip(m, 0, n_items - 1)
        return qi_tab[m], kj_tab[m]

    def step(n, slot, carries, accs):
        prev = 1 - slot
        qi1, kj1 = item(n)
        kb = kn_ref[pl.ds(pl.multiple_of(kj1 * tk, tk), tk), :]
        for h in range(2):
            qh = qh_ref[h, pl.ds(pl.multiple_of(qi1 * tq, tq), tq), :]
            z_ref[slot, h] = lax.dot_general(qh, kb, (((1,), (1,)), ((), ())), preferred_element_type=F32)
        qi2, kj2 = item(n - 1)
        keep = rel < (qi2 * tq - kj2 * tk)
        for h in range(2):
            z = jnp.where(keep, z_ref[prev, h], -MASKED)
            e = jnp.exp2(jnp.abs(z) * (-LOG2E))
            r = pl.reciprocal(1.0 + e)
            sp =jnp.maximum(z, 0.0) - jnp.log(r)
            beta = jnp.where(z >= 0.0, r, e * r)
            sp_ref[slot, h] = sp.astype(BF16)
            bt_ref[slot, h] = beta.astype(BF16)
            rs_ref[slot, h] = jnp.broadcast_to(jnp.sum(sp, axis=1, keepdims=True), (tq, LANES))
        qi3, kj3 = item(n - 2)
        first3 = kj3 == (qi3 + 1) * n_diag - 1
        new_carries = []
        for h in range(2):
            t = jnp.dot(sp_ref[prev, h], u_strict, preferred_element_type=F32)
            c_in = jnp.where(first3, 0.0, carries[h])
            c_wide = jnp.concatenate([c_in] * (tk // LANES), axis=1)
            w = jnp.exp2((t + c_wide) * LOG2E) * bt_ref[prev, h].astype(F32)
            w_ref[slot, h] = w.astype(BF16)
            new_carries.append(c_in - rs_ref[prev, h])
        qi4, kj4 = item(n - 3)
        first4 = kj4 == (qi4 + 1) * n_diag - 1
        vb = v_ref[0, pl.ds(pl.multiple_of(kj4 * tk, tk), tk), :]
        new_accs = []
        for h in range(2):
            pv = jnp.dot(w_ref[prev, h], vb, preferred_element_type=F32)
            new_accs.append(jnp.where(first4, 0.0, accs[h]) + pv)
        o_ref[0, pl.ds(pl.multiple_of(qi4 * tq, tq), tq), :] = (
            jnp.where(out_lane, new_accs[0], new_accs[1]).astype(o_ref.dtype))
        return new_carries, new_accs

    def body(n2, st):
        carries, accs = [st[0], st[1]], [st[2], st[3]]
        carries, accs = step(2 * n2, 0, carries, accs)
        carries, accs = step(2 * n2 + 1, 1, carries, accs)
        return (carries[0], carries[1], accs[0], accs[1])

    zc = jnp.zeros((tq, LANES), F32)
    za = jnp.zeros((tq, LANES), F32)
    n_steps = n_items + 3
    assert n_steps % 2 == 0
    lax.fori_loop(0, n_steps // 2, body, (zc, zc, za, za))


def sb_attention(proj, q_gain, k_gain, *, batch, seq, tq, tk):
    tq = min(tq, seq)
    tk = min(tk, tq)
    assert seq % tq == 0 and tq % tk == 0
    n_pairs = SB_HEADS // 2
    qi, kj = _sb_items(seq // tq, tq // tk)
    g2 = lambda g: jnp.concatenate([g, g]).reshape(1, LANES).astype(F32)
    seq_blk = lambda s0: pl.BlockSpec((1, seq, LANES), lambda b, p, qt, kt: (s0 + p, b, 0))
    stage = lambda dt: pltpu.VMEM((2, 2, tq, tk), dt)
    return pl.pallas_call(
        functools.partial(_sb_kernel, tq=tq, tk=tk, seq=seq, n_items=len(qi)),
        grid_spec=pltpu.PrefetchScalarGridSpec(
            num_scalar_prefetch=2,
            grid=(batch, n_pairs),
            in_specs=[
                seq_blk(SLAB_SB_Q), seq_blk(SLAB_SB_K), seq_blk(SLAB_SB_V),
                pl.BlockSpec((1, LANES), lambda b, p, qt, kt: (0, 0)),
                pl.BlockSpec((1, LANES), lambda b, p, qt, kt: (0, 0)),
            ],
            out_specs=pl.BlockSpec((1, seq, LANES), lambda b, p, qt, kt: (p, b, 0)),
            scratch_shapes=[
                pltpu.VMEM((2, seq, LANES), BF16),
                pltpu.VMEM((seq, LANES), BF16),
                stage(F32), stage(BF16), stage(BF16), stage(BF16),
                pltpu.VMEM((2, 2, tq, LANES), F32),
            ],
        ),
        out_shape=jax.ShapeDtypeStruct((n_pairs, batch * seq, LANES), BF16),
        compiler_params=_params("parallel", "parallel"),
        name="sb_attention",
    )(jnp.asarray(qi, jnp.int32), jnp.asarray(kj, jnp.int32), proj, proj, proj, g2(q_gain), g2(k_gain))


GLA_ROWS = 256
GLA_KW = GLA_HEADS * GLA_DK


def _gla_kernel(q_ref, k_ref, v_ref, a_ref, r_ref, aup_ref, ab_ref, og_ref, o_ref, state_ref):
    rows = GLA_ROWS
    n_chunks = rows // GLA_CHUNK

    @pl.when(pl.program_id(1) == 0)
    def _():
        state_ref[...] = jnp.zeros_like(state_ref)

    pre = jnp.dot(a_ref[0], aup_ref[...], preferred_element_type=F32) + ab_ref[...]
    log_a = -_softplus(-pre) * (1.0 / GLA_GATE_TAU)
    la_hi, la_lo = _split_bf16(log_a)

    ri = lax.broadcasted_iota(jnp.int32, (rows, rows), 0)
    ci = lax.broadcasted_iota(jnp.int32, (rows, rows), 1)
    same_chunk = (ri // GLA_CHUNK) == (ci // GLA_CHUNK)
    tri = same_chunk & (ri >= ci)
    tri_b = jnp.where(tri, 1.0, 0.0).astype(BF16)
    ones_b = jnp.where(same_chunk, 1.0, 0.0).astype(BF16)
    cum = (jnp.dot(tri_b, la_hi, preferred_element_type=F32)
           + jnp.dot(tri_b, la_lo, preferred_element_type=F32))
    cum_last = (jnp.dot(ones_b, la_hi, preferred_element_type=F32)
                + jnp.dot(ones_b, la_lo, preferred_element_type=F32))

    q_all = jnp.concatenate([q_ref[0], q_ref[1]], axis=-1).astype(F32)
    k_all = jnp.concatenate([k_ref[0], k_ref[1]], axis=-1).astype(F32)
    q_dec = q_all * (GLA_DK ** -0.5) * jnp.exp(cum)
    k_inv = (k_all * jnp.exp(-cum)).astype(BF16)
    k_end = (k_all * jnp.exp(cum_last - cum)).astype(BF16)
    v_all = jnp.concatenate([v_ref[h] for h in range(GLA_HEADS)], axis=-1)

    lane = lax.broadcasted_iota(jnp.int32, (rows, GLA_KW), 1)
    q_heads = [jnp.where((lane // GLA_DK) == h, q_dec, 0.0).astype(BF16) for h in range(GLA_HEADS)]

    o_heads = []
    for h in range(GLA_HEADS):
        sc = lax.dot_general(q_heads[h], k_inv, (((1,), (1,)), ((), ())), preferred_element_type=F32)
        sc = jnp.where(tri, sc, 0.0).astype(BF16)
        o_heads.append(jnp.dot(sc, v_ref[h], preferred_element_type=F32))

    ones_cols = jnp.ones((GLA_CHUNK, LANES), BF16)
    state = state_ref[...]
    inter = [[] for _ in range(GLA_HEADS)]
    for c in range(n_chunks):
        r0, r1 = c * GLA_CHUNK, (c + 1) * GLA_CHUNK
        state_b = state.astype(BF16)
        for h in range(GLA_HEADS):
            inter[h].append(jnp.dot(q_heads[h][r0:r1], state_b, preferred_element_type=F32))
        kv = lax.dot_general(k_end[r0:r1], v_all[r0:r1], (((0,), (0,)), ((), ())),
                             preferred_element_type=F32)
        kv_sel = jnp.concatenate(
            [kv[h * GLA_DK:(h + 1) * GLA_DK, h * GLA_DV:(h + 1) * GLA_DV] for h in range(GLA_HEADS)], axis=0)
        tot = (lax.dot_general(la_hi[r0:r1], ones_cols, (((0,), (0,)), ((), ())), preferred_element_type=F32)
               + lax.dot_general(la_lo[r0:r1], ones_cols, (((0,), (0,)), ((), ())), preferred_element_type=F32))
        state = jnp.exp(tot) * state + kv_sel
    state_ref[...] = state

    for h in range(GLA_HEADS):
        o = o_heads[h] + jnp.concatenate(inter[h], axis=0)
        o = _rms_rows(o) * og_ref[...]
        r = r_ref[h].astype(F32)
        o_ref[h] = (o * (r * jax.nn.sigmoid(r))).astype(o_ref.dtype)


def gla(proj, a_up, a_b, o_gain, *, batch, seq):
    rows = GLA_ROWS
    assert seq % rows == 0
    nb = seq // rows
    a_up_p = jnp.zeros((LANES, GLA_KW), BF16).at[:GLA_GATE_RANK].set(a_up.astype(BF16))
    row_blk = lambda n, s0: pl.BlockSpec((n, rows, LANES), lambda b, i: (s0 // n, b * nb + i, 0))
    return pl.pallas_call(
        _gla_kernel,
        grid=(batch, nb),
        in_specs=[
            row_blk(2, SLAB_GLA_Q), row_blk(2, SLAB_GLA_K), row_blk(4, SLAB_GLA_V),
            row_blk(1, SLAB_GLA_A), row_blk(4, SLAB_GLA_R),
            pl.BlockSpec((LANES, GLA_KW), lambda b, i: (0, 0)),
            pl.BlockSpec((1, GLA_KW), lambda b, i: (0, 0)),
            pl.BlockSpec((1, GLA_DV), lambda b, i: (0, 0)),
        ],
        out_specs=pl.BlockSpec((GLA_HEADS, rows, LANES), lambda b, i: (0, b * nb + i, 0)),
        out_shape=jax.ShapeDtypeStruct((GLA_HEADS, batch * seq, LANES), BF16),
        scratch_shapes=[pltpu.VMEM((GLA_KW, GLA_DV), F32)],
        compiler_params=_params("parallel", "arbitrary"),
        name="gla",
    )(proj, proj, proj, proj, proj, a_up_p,
      a_b.reshape(1, GLA_KW).astype(F32), o_gain.reshape(1, GLA_DV).astype(F32))


def _merge_kernel(x_ref, oa_ref, ob_ref, cq_ref, gate_ref, mkv_ref, cqg_ref, ckg_ref,
                  wsb_ref, wgla_ref, wca_ref, wout_ref, o_ref):
    ca_w = CA_HEADS * CA_HEAD_DIM
    oc = []
    for h in range(CA_HEADS):
        q = _rms_rows(cq_ref[h].astype(F32)) * (cqg_ref[...] * CA_HEAD_DIM ** -0.5)
        kk = _rms_rows(mkv_ref[:, h * CA_HEAD_DIM:(h + 1) * CA_HEAD_DIM].astype(F32)) * ckg_ref[...]
        s = lax.dot_general(q.astype(BF16), kk.astype(BF16), (((1,), (1,)), ((), ())),
                            preferred_element_type=F32)
        e = jnp.exp(s - jnp.max(s, axis=-1, keepdims=True))
        p = e * (1.0 / jnp.sum(e, axis=-1, keepdims=True))
        vv = mkv_ref[:, ca_w + h * CA_HEAD_DIM: ca_w + (h + 1) * CA_HEAD_DIM]
        oc.append(jnp.dot(p.astype(BF16), vv, preferred_element_type=F32).astype(BF16))
    o_c = jnp.concatenate(oc, axis=-1)
    o_a = jnp.concatenate([oa_ref[s] for s in range(4)], axis=-1)
    o_b = jnp.concatenate([ob_ref[s] for s in range(4)], axis=-1)

    def gate(b):
        g = jnp.concatenate([gate_ref[b * 8 + s] for s in range(8)], axis=-1).astype(F32)
        return jax.nn.sigmoid(g)

    merged = gate(0) * jnp.dot(o_a, wsb_ref[...], preferred_element_type=F32)
    merged += gate(1) * jnp.dot(o_b, wgla_ref[...], preferred_element_type=F32)
    merged += gate(2) * jnp.dot(o_c, wca_ref[...], preferred_element_type=F32)
    o_ref[...] = x_ref[...] + jnp.dot(merged.astype(BF16), wout_ref[...], preferred_element_type=F32)


def merge_branches(x, o_a, o_b, proj, mkv, cq_gain, ck_gain, w_sb, w_gla, w_ca, w_out, *, batch, seq, tm):
    n, d = x.shape
    tm = min(tm, seq)
    assert seq % tm == 0
    nt = seq // tm
    mem_len = mkv.shape[0] // batch
    full = lambda a: pl.BlockSpec(a.shape, lambda i: (0,) * a.ndim)
    cqg = cq_gain.reshape(1, CA_HEAD_DIM).astype(F32)
    ckg = ck_gain.reshape(1, CA_HEAD_DIM).astype(F32)
    return pl.pallas_call(
        _merge_kernel,
        grid=(n // tm,),
        in_specs=[
            pl.BlockSpec((tm, d), lambda i: (i, 0)),
            pl.BlockSpec((4, tm, LANES), lambda i: (0, i, 0)),
            pl.BlockSpec((4, tm, LANES), lambda i: (0, i, 0)),
            pl.BlockSpec((4, tm, LANES), lambda i: (SLAB_CA_Q // 4, i, 0)),
            pl.BlockSpec((24, tm, LANES), lambda i: (SLAB_GATES // 24, i, 0)),
            pl.BlockSpec((mem_len, mkv.shape[1]), lambda i: (i // nt, 0)),
            full(cqg), full(ckg), full(w_sb), full(w_gla), full(w_ca), full(w_out),
        ],
        out_specs=pl.BlockSpec((tm, d), lambda i: (i, 0)),
        out_shape=jax.ShapeDtypeStruct((n, d), F32),
        compiler_params=_params("parallel"),
        name="merge_branches",
    )(x, o_a, o_b, proj, proj, mkv,
      cqg, ckg, w_sb, w_gla, w_ca, w_out)


def _ffn_kernel(x_ref, g_ref, wg_ref, wu_ref, wd_ref, o_ref, hn_ref, acc_ref):
    f = pl.program_id(1)

    @pl.when(f == 0)
    def _():
        hn_ref[...] = (_rms_rows(x_ref[...]) * g_ref[...]).astype(BF16)
        acc_ref[...] = jnp.zeros_like(acc_ref)

    h = hn_ref[...]
    gt = jnp.dot(h, wg_ref[...], preferred_element_type=F32)
    up = jnp.dot(h, wu_ref[...], preferred_element_type=F32)
    act = (gt * jax.nn.sigmoid(gt) * up).astype(BF16)
    acc_ref[...] += jnp.dot(act, wd_ref[...], preferred_element_type=F32)

    @pl.when(f == pl.num_programs(1) - 1)
    def _():
        o_ref[...] = x_ref[...] + acc_ref[...]


def ffn_dense(x, g, w_gate_up, w_down, *, tm, tf):
    n, d = x.shape
    d_ff = w_down.shape[0]
    tm = min(tm, n)
    assert n % tm == 0 and d_ff % tf == 0 and tf % LANES == 0
    nf = d_ff // tf
    return pl.pallas_call(
        _ffn_kernel,
        grid=(n // tm, nf),
        in_specs=[
            pl.BlockSpec((tm, d), lambda i, f: (i, 0)),
            pl.BlockSpec((1, d), lambda i, f: (0, 0)),
            pl.BlockSpec((d, tf), lambda i, f: (0, f)),
            pl.BlockSpec((d, tf), lambda i, f: (0, nf + f)),
            pl.BlockSpec((tf, d), lambda i, f: (f, 0)),
        ],
        out_specs=pl.BlockSpec((tm, d), lambda i, f: (i, 0)),
        out_shape=jax.ShapeDtypeStruct((n, d), F32),
        scratch_shapes=[pltpu.VMEM((tm, d), BF16), pltpu.VMEM((tm, d), F32)],
        compiler_params=_params("parallel", "arbitrary"),
        name="ffn_dense",
    )(x, g.reshape(1, d), w_gate_up, w_gate_up, w_down)


def _router_kernel(x_ref, g_ref, wr_ref, c_ref):
    hn = _rms_rows(x_ref[...]) * g_ref[...]
    logits = jnp.dot(hn, wr_ref[...], preferred_element_type=F32, precision=lax.Precision.HIGHEST)
    lane = lax.broadcasted_iota(jnp.int32, logits.shape, 1).astype(F32)
    neg = jnp.float32(-jnp.inf)
    logits = jnp.where(lane < N_EXPERTS, logits, neg)
    m1 = jnp.max(logits, axis=-1, keepdims=True)
    i1 = jnp.min(jnp.where(logits == m1, lane, float(LANES)), axis=-1, keepdims=True)
    rest = jnp.where(lane == i1, neg, logits)
    m2 = jnp.max(rest, axis=-1, keepdims=True)
    i2 = jnp.min(jnp.where(rest == m2, lane, float(LANES)), axis=-1, keepdims=True)
    e2 = jnp.exp(m2 - m1)
    inv = 1.0 / (1.0 + e2)
    c_ref[...] = (jnp.where(lane == 0.0, inv, 0.0) + jnp.where(lane == 1.0, e2 * inv, 0.0)
                  + jnp.where(lane == 2.0, i1, 0.0) + jnp.where(lane == 3.0, i2, 0.0))


def router(x, g, w_router, *, tm):
    n, d = x.shape
    tm = min(tm, n)
    wr = jnp.zeros((d, LANES), F32).at[:, :N_EXPERTS].set(w_router)
    return pl.pallas_call(
        _router_kernel,
        grid=(n // tm,),
        in_specs=[
            pl.BlockSpec((tm, d), lambda i: (i, 0)),
            pl.BlockSpec((1, d), lambda i: (0, 0)),
            pl.BlockSpec((d, LANES), lambda i: (0, 0)),
        ],
        out_specs=pl.BlockSpec((tm, LANES), lambda i: (i, 0)),
        out_shape=jax.ShapeDtypeStruct((n, LANES), F32),
        compiler_params=_params("parallel"),
        name="router",
    )(x, g.reshape(1, d), wr)


def _row_copy(src_hbm, row, dst_vmem, r, sem):
    return pltpu.make_async_copy(src_hbm.at[pl.ds(row, 1)], dst_vmem.at[pl.ds(r, 1)], sem)


def _expert_ffn_kernel(te_ref, nu_ref, cur_ref, nxt_ref, x_hbm, g_ref, wg_ref, wu_ref, wd_ref, o_ref,
                       xbuf_ref, sem, hn_ref, acc_ref):
    i = pl.program_id(0)
    f = pl.program_id(1)
    rows = o_ref.shape[0]
    used = i < nu_ref[0]
    slot = i % 2

    def issue(idx_ref, s):
        def one(r, carry):
            _row_copy(x_hbm, idx_ref[0, 0, r], xbuf_ref.at[s], r, sem.at[s]).start()
            return carry
        lax.fori_loop(0, rows, one, 0, unroll=GATHER_UNROLL)

    @pl.when((i == 0) & (f == 0))
    def _():
        issue(cur_ref, 0)

    @pl.when(used & (f == 0))
    def _():
        def one(r, carry):
            _row_copy(x_hbm, 0, xbuf_ref.at[slot], r, sem.at[slot]).wait()
            return carry
        lax.fori_loop(0, rows, one, 0, unroll=GATHER_UNROLL)

        @pl.when(i + 1 < nu_ref[0])
        def _():
            issue(nxt_ref, 1 - slot)

        hn_ref[...] = (_rms_rows(xbuf_ref[slot]) * g_ref[...]).astype(BF16)
        acc_ref[...] = jnp.zeros_like(acc_ref)

    @pl.when(used)
    def _():
        h = hn_ref[...]
        gt = jnp.dot(h, wg_ref[0], preferred_element_type=F32)
        up = jnp.dot(h, wu_ref[0], preferred_element_type=F32)
        act = (gt * jax.nn.sigmoid(gt) * up).astype(BF16)
        acc_ref[...] += jnp.dot(act, wd_ref[0], preferred_element_type=F32)

    @pl.when(f == pl.num_programs(1) - 1)
    def _():
        o_ref[...] = jnp.where(used, acc_ref[...], 0.0)


def expert_ffn(x, src, g, tile_expert, n_used, w_gate_up, w_down, *, tm, tf):
    rows = src.shape[0]
    d = x.shape[1]
    d_ff = w_down.shape[1]
    assert rows % tm == 0 and d_ff % tf == 0 and tf % LANES == 0
    nf = d_ff // tf
    n_tiles = rows // tm
    src = src.reshape(n_tiles, 1, tm)
    return pl.pallas_call(
        _expert_ffn_kernel,
        grid_spec=pltpu.PrefetchScalarGridSpec(
            num_scalar_prefetch=2,
            grid=(n_tiles, nf),
            in_specs=[
                pl.BlockSpec((1, 1, tm), lambda i, f, te, nu: (i, 0, 0), memory_space=pltpu.SMEM),
                pl.BlockSpec((1, 1, tm), lambda i, f, te, nu: (jnp.minimum(i + 1, n_tiles - 1), 0, 0),
                             memory_space=pltpu.SMEM),
                pl.BlockSpec(memory_space=pl.ANY),
                pl.BlockSpec((1, d), lambda i, f, te, nu: (0, 0)),
                pl.BlockSpec((1, d, tf), lambda i, f, te, nu: (te[i], 0, f)),
                pl.BlockSpec((1, d, tf), lambda i, f, te, nu: (te[i], 0, nf + f)),
                pl.BlockSpec((1, tf, d), lambda i, f, te, nu: (te[i], f, 0)),
            ],
            out_specs=pl.BlockSpec((tm, d), lambda i, f, te, nu: (i, 0)),
            scratch_shapes=[pltpu.VMEM((2, tm, d), F32), pltpu.SemaphoreType.DMA((2,)),
                            pltpu.VMEM((tm, d), BF16), pltpu.VMEM((tm, d), F32)],
        ),
        out_shape=jax.ShapeDtypeStruct((rows, d), F32),
        compiler_params=_params("arbitrary", "arbitrary"),
        name="expert_ffn",
    )(tile_expert, n_used, src, src, x, g.reshape(1, d), w_gate_up, w_gate_up, w_down)


def _combine_kernel(p1_ref, p2_ref, n1_ref, n2_ref, x_ref, rw_ref, ys_hbm, o_ref, a_ref, b_ref, sem):
    i = pl.program_id(0)
    rows = o_ref.shape[0]
    slot = i % 2

    def issue(q1_ref, q2_ref, s):
        def one(r, carry):
            _row_copy(ys_hbm, q1_ref[0, 0, r], a_ref.at[s], r, sem.at[0, s]).start()
            _row_copy(ys_hbm, q2_ref[0, 0, r], b_ref.at[s], r, sem.at[1, s]).start()
            return carry
        lax.fori_loop(0, rows, one, 0, unroll=GATHER_UNROLL)

    @pl.when(i == 0)
    def _():
        issue(p1_ref, p2_ref, 0)

    @pl.when(i + 1 < pl.num_programs(0))
    def _():
        issue(n1_ref, n2_ref, 1 - slot)

    def drain(r, carry):
        _row_copy(ys_hbm, 0, a_ref.at[slot], r, sem.at[0, slot]).wait()
        _row_copy(ys_hbm, 0, b_ref.at[slot], r, sem.at[1, slot]).wait()
        return carry
    lax.fori_loop(0, rows, drain, 0, unroll=GATHER_UNROLL)

    rw = rw_ref[...]
    o_ref[...] = x_ref[...] + (rw[:, 0:1] * a_ref[slot] + rw[:, 1:2] * b_ref[slot])


def moe_combine(x, route, ys, pos1, pos2, *, tc):
    n, d = x.shape
    assert n % tc == 0
    nt = n // tc
    idx_spec = pl.BlockSpec((1, 1, tc), lambda i: (i, 0, 0), memory_space=pltpu.SMEM)
    nxt_spec = pl.BlockSpec((1, 1, tc), lambda i: (jnp.minimum(i + 1, nt - 1), 0, 0), memory_space=pltpu.SMEM)
    pos1 = pos1.reshape(nt, 1, tc)
    pos2 = pos2.reshape(nt, 1, tc)
    return pl.pallas_call(
        _combine_kernel,
        grid=(nt,),
        in_specs=[
            idx_spec, idx_spec, nxt_spec, nxt_spec,
            pl.BlockSpec((tc, d), lambda i: (i, 0)),
            pl.BlockSpec((tc, LANES), lambda i: (i, 0)),
            pl.BlockSpec(memory_space=pl.ANY),
        ],
        out_specs=pl.BlockSpec((tc, d), lambda i: (i, 0)),
        out_shape=jax.ShapeDtypeStruct((n, d), F32),
        scratch_shapes=[pltpu.VMEM((2, tc, d), F32), pltpu.VMEM((2, tc, d), F32),
                        pltpu.SemaphoreType.DMA((2, 2))],
        compiler_params=_params("arbitrary"),
        name="moe_combine",
    )(pos1, pos2, pos1, pos2, x, route, ys)


def _dispatch_plan(route, *, tm):
    n = route.shape[0]
    ids = route[:, 2:4].astype(jnp.int32)
    e_flat = ids.T.reshape(-1)
    onehot = (e_flat[:, None] == jnp.arange(N_EXPERTS)[None, :]).astype(jnp.int32)
    running = jnp.cumsum(onehot, axis=0)
    rank = jnp.sum(running * onehot, axis=1) - 1
    counts = running[-1]
    padded = ((counts + tm - 1) // tm) * tm
    ends = jnp.cumsum(padded)
    starts = ends - padded
    pos = starts[e_flat] + rank
    rows = TOP_K * n + N_EXPERTS * tm
    tok = jnp.tile(jnp.arange(n, dtype=jnp.int32), TOP_K)
    src = jnp.zeros((rows,), jnp.int32).at[pos].set(tok)
    tile_start = jnp.arange(rows // tm, dtype=jnp.int32) * tm
    tile_expert = jnp.minimum(jnp.sum(tile_start[:, None] >= ends[None, :], axis=1), N_EXPERTS - 1)
    n_used = (ends[-1] // tm).reshape(1).astype(jnp.int32)
    return src, tile_expert.astype(jnp.int32), n_used, pos[:n].astype(jnp.int32), pos[n:].astype(jnp.int32)


def ffn_moe(x, g, route, w_gate_up, w_down, *, tm, tf):
    src, tile_expert, n_used, pos1, pos2 = _dispatch_plan(route, tm=tm)
    ys = expert_ffn(x, src, g, tile_expert, n_used, w_gate_up, w_down, tm=tm, tf=tf)
    return moe_combine(x, route, ys, pos1, pos2, tc=min(256, x.shape[0]))


def _slab_weights(w_in):
    d = w_in.shape[0]
    sbw = SB_HEADS * SB_HEAD_DIM
    glv = GLA_HEADS * GLA_DV
    caw = CA_HEADS * CA_HEAD_DIM
    sizes = (sbw, sbw, sbw, GLA_KW, GLA_KW, glv, GLA_GATE_RANK, glv, caw, N_BRANCH * D_MODEL)
    parts, off = [], 0
    for s in sizes:
        parts.append(w_in[:, off:off + s])
        off += s
    sb_q, sb_k, sb_v, g_q, g_k, g_v, g_a, g_r, ca_q, gates = parts
    pad = lambda cols: jnp.zeros((d, cols), w_in.dtype)
    cols = [sb_q, sb_k, sb_v, g_v, g_r, ca_q, gates, g_q, g_k, g_a, pad(2 * LANES - GLA_GATE_RANK)]
    w = jnp.concatenate(cols, axis=1).astype(BF16)
    assert w.shape[1] == N_SLABS * LANES
    return w


def kernel(x, mem, mix_norm_g, w_in, sb_q_g, sb_k_g, gla_a_up, gla_a_b, gla_o_g, ca_q_g, ca_k_g, mem_norm_g, w_mem_kv, w_br_sb, w_br_gla, w_br_ca, w_out, ffn_norm_g, ff_w_gate_up, ff_w_down, moe_w_router, moe_w_gate_up, moe_w_down):
    batch, seq, d = x.shape
    depth = w_in.shape[0]
    n = batch * seq
    xf = x.reshape(n, d)
    memf = mem.reshape(-1, d)
    bf = lambda a: a.astype(BF16)

    for layer in range(depth):
        proj = norm_matmul(xf, mix_norm_g[layer], _slab_weights(w_in[layer]), tm=512, tn=2304, slabs=True)
        mkv = norm_matmul(memf, mem_norm_g[layer], bf(w_mem_kv[layer]), tm=512, tn=1024, slabs=False)
        o_a = sb_attention(proj, sb_q_g[layer], sb_k_g[layer], batch=batch, seq=seq, tq=256, tk=256)
        o_b = gla(proj, gla_a_up[layer], gla_a_b[layer], gla_o_g[layer], batch=batch, seq=seq)
        xf = merge_branches(xf, o_a, o_b, proj, mkv, ca_q_g[layer], ca_k_g[layer],
                            bf(w_br_sb[layer]), bf(w_br_gla[layer]), bf(w_br_ca[layer]), bf(w_out[layer]),
                            batch=batch, seq=seq, tm=512)
        j = layer // 2
        if layer % 2 == 0:
            xf = ffn_dense(xf, ffn_norm_g[layer], bf(ff_w_gate_up[j]), bf(ff_w_down[j]), tm=512, tf=1408)
        else:
            route = router(xf, ffn_norm_g[layer], moe_w_router[j], tm=512)
            xf = ffn_moe(xf, ffn_norm_g[layer], route, bf(moe_w_gate_up[j]), bf(moe_w_down[j]), tm=512, tf=896)
    return xf.reshape(batch, seq, d)
```

```python
import functools

import jax
import jax.numpy as jnp
from jax import lax
from jax.experimental import pallas as pl
from jax.experimental.pallas import tpu as pltpu

F32 = jnp.float32
BF16 = jnp.bfloat16

LANES = 128
EPS = 1e-6
LOG2E = 1.4426950408889634
LN2 = 0.6931471805599453
MASKED = 1e30
GATHER_UNROLL = 8
SB_STAGES = 4
SB_EXIT = -110.0
VMEM_LIMIT_BYTES = 56 * 1024 * 1024

D_MODEL = 1024
SB_HEADS, SB_HEAD_DIM = 8, 64
GLA_HEADS, GLA_DK, GLA_DV = 4, 64, 128
GLA_GATE_RANK, GLA_GATE_TAU, GLA_CHUNK = 16, 16.0, 64
CA_HEADS, CA_HEAD_DIM = 4, 128
N_BRANCH = 3
N_EXPERTS, TOP_K = 8, 2

SLAB_SB_Q, SLAB_SB_K, SLAB_SB_V = 0, 4, 8
SLAB_GLA_V, SLAB_GLA_R = 12, 16
SLAB_CA_Q = 20
SLAB_GATES = 24
SLAB_GLA_Q, SLAB_GLA_K, SLAB_GLA_A = 48, 50, 52
N_SLABS = 54


def _params(*semantics):
    return pltpu.CompilerParams(dimension_semantics=semantics, vmem_limit_bytes=VMEM_LIMIT_BYTES)


def _rms_rows(x, eps=EPS):
    return x * lax.rsqrt(jnp.mean(x * x, axis=-1, keepdims=True) + eps)


def _rms_half_lanes(x, eps=EPS):
    lane = lax.broadcasted_iota(jnp.int32, x.shape, 1)
    lo = lane < SB_HEAD_DIM
    sq = x * x
    s_lo = jnp.sum(jnp.where(lo, sq, 0.0), axis=-1, keepdims=True)
    s_hi = jnp.sum(jnp.where(lo, 0.0, sq), axis=-1, keepdims=True)
    ms = jnp.where(lo, s_lo, s_hi) * (1.0 / SB_HEAD_DIM)
    return x * lax.rsqrt(ms + eps)


def _split_bf16(x):
    hi = x.astype(BF16)
    lo = (x - hi.astype(F32)).astype(BF16)
    return hi, lo


def _norm_matmul_kernel(x_ref, g_ref, w_ref, o_ref, xn_ref, *, n_slab):
    @pl.when(pl.program_id(1) == 0)
    def _():
        xn_ref[...] = (_rms_rows(x_ref[...]) * g_ref[...]).astype(BF16)

    acc = jnp.dot(xn_ref[...], w_ref[...], preferred_element_type=F32)
    if n_slab is None:
        o_ref[...] = acc.astype(o_ref.dtype)
    else:
        for s in range(n_slab):
            o_ref[s] = acc[:, s * LANES:(s + 1) * LANES].astype(o_ref.dtype)


def norm_matmul(x, g, w, *, tm, tn, slabs):
    n, k = x.shape
    m = w.shape[1]
    tm = min(tm, n)
    tn = min(tn, m)
    assert n % tm == 0 and m % tn == 0 and tn % LANES == 0
    if slabs:
        n_slab = tn // LANES
        out_shape = jax.ShapeDtypeStruct((m // LANES, n, LANES), BF16)
        out_spec = pl.BlockSpec((n_slab, tm, LANES), lambda i, j: (j, i, 0))
    else:
        n_slab = None
        out_shape = jax.ShapeDtypeStruct((n, m), BF16)
        out_spec = pl.BlockSpec((tm, tn), lambda i, j: (i, j))
    return pl.pallas_call(
        functools.partial(_norm_matmul_kernel, n_slab=n_slab),
        grid=(n // tm, m // tn),
        in_specs=[
            pl.BlockSpec((tm, k), lambda i, j: (i, 0)),
            pl.BlockSpec((1, k), lambda i, j: (0, 0)),
            pl.BlockSpec((k, tn), lambda i, j: (0, j)),
        ],
        out_specs=out_spec,
        out_shape=out_shape,
        scratch_shapes=[pltpu.VMEM((tm, k), BF16)],
        compiler_params=_params("parallel", "arbitrary"),
        name="norm_matmul",
    )(x, g.reshape(1, k), w)


def _softplus(z):
    return jnp.maximum(z, 0.0) + jnp.log(1.0 + jnp.exp(-jnp.abs(z)))


def _sb_items(nq, n_diag):
    qi, kj = [], []
    for i in range(nq):
        for j in reversed(range((i + 1) * n_diag)):
            qi.append(i)
            kj.append(j)
    n_items = len(qi)
    tile_start = [i * (i + 1) // 2 * n_diag for i in range(nq + 1)]
    assert tile_start[nq] == n_items
    qi += [0] * n_diag * SB_STAGES
    kj += list(reversed(range(n_diag))) * SB_STAGES
    return n_items, qi, kj, tile_start


def _sb_kernel(qi_tab, kj_tab, ts_tab, q_ref, k_ref, v_ref, qg_ref, kg_ref, o_ref,
               qh_ref, kn_ref, z_ref, sp_ref, bt_ref, w_ref, rs_ref, *, tq, tk, seq, n_items):
    n_diag = tq // tk
    rows = min(512, seq)
    lane = lax.broadcasted_iota(jnp.int32, (rows, LANES), 1)
    scale = SB_HEAD_DIM ** -0.5

    def norm_chunk(c, carry):
        r0 = pl.multiple_of(c * rows, rows)
        kk = k_ref[0, pl.ds(r0, rows), :].astype(F32)
        kn_ref[pl.ds(r0, rows), :] = (_rms_half_lanes(kk) * kg_ref[...]).astype(BF16)
        qn = _rms_half_lanes(q_ref[0, pl.ds(r0, rows), :].astype(F32)) * (qg_ref[...] * scale)
        qh_ref[0, pl.ds(r0, rows), :] = jnp.where(lane < SB_HEAD_DIM, qn, 0.0).astype(BF16)
        qh_ref[1, pl.ds(r0, rows), :] = jnp.where(lane >= SB_HEAD_DIM, qn, 0.0).astype(BF16)
        return carry
    lax.fori_loop(0, seq // rows, norm_chunk, 0)

    z_ref[...] = jnp.zeros_like(z_ref)
    sp_ref[...] = jnp.zeros_like(sp_ref)
    bt_ref[...] = jnp.zeros_like(bt_ref)
    w_ref[...] = jnp.zeros_like(w_ref)
    rs_ref[...] = jnp.zeros_like(rs_ref)
    o_ref[...] = jnp.zeros_like(o_ref)

    rj = lax.broadcasted_iota(jnp.int32, (tk, tk), 0)
    cs = lax.broadcasted_iota(jnp.int32, (tk, tk), 1)
    u_strict = jnp.where(rj > cs, -1.0, 0.0).astype(BF16)
    rel = (lax.broadcasted_iota(jnp.int32, (tq, tk), 1) - lax.broadcasted_iota(jnp.int32, (tq, tk), 0))
    out_lane = lax.broadcasted_iota(jnp.int32, (tq, LANES), 1) < SB_HEAD_DIM

    def item(m):
        return qi_tab[m], kj_tab[m]

    def step(slot, ids, carries, accs):
        prev = 1 - slot
        qi1, kj1 = item(ids[0])
        kb = kn_ref[pl.ds(pl.multiple_of(kj1 * tk, tk), tk), :]
        for h in range(2):
            qh = qh_ref[h, pl.ds(pl.multiple_of(qi1 * tq, tq), tq), :]
            z_ref[slot, h] = lax.dot_general(qh, kb, (((1,), (1,)), ((), ())), preferred_element_type=F32)
        qi2, kj2 = item(ids[1])
        keep = rel < (qi2 * tq - kj2 * tk)
        for h in range(2):
            z = jnp.where(keep, z_ref[prev, h], -MASKED)
            e = jnp.exp2(jnp.abs(z) * (-LOG2E))
            r = pl.reciprocal(1.0 + e)
            sp =jnp.maximum(z, 0.0) - jnp.log(r)
            beta = jnp.where(z >= 0.0, r, e * r)
            sp_ref[slot, h] = sp.astype(BF16)
            bt_ref[slot, h] = beta.astype(BF16)
            rs_ref[slot, h] = jnp.broadcast_to(jnp.sum(sp, axis=1, keepdims=True), (tq, LANES))
        qi3, kj3 = item(ids[2])
        first3 = kj3 == (qi3 + 1) * n_diag - 1
        new_carries = []
        for h in range(2):
            t = jnp.dot(sp_ref[prev, h], u_strict, preferred_element_type=F32)
            c_in = jnp.where(first3, 0.0, carries[h])
            c_wide = jnp.concatenate([c_in] * (tk // LANES), axis=1)
            w = jnp.exp2((t + c_wide) * LOG2E) * bt_ref[prev, h].astype(F32)
            w_ref[slot, h] = w.astype(BF16)
            new_carries.append(c_in - rs_ref[prev, h])
        qi4, kj4 = item(ids[3])
        first4 = kj4 == (qi4 + 1) * n_diag - 1
        vb = v_ref[0, pl.ds(pl.multiple_of(kj4 * tk, tk), tk), :]
        new_accs = []
        for h in range(2):
            pv = jnp.dot(w_ref[prev, h], vb, preferred_element_type=F32)
            new_accs.append(jnp.where(first4, 0.0, accs[h]) + pv)
        out_rows = pl.ds(pl.multiple_of(qi4 * tq, tq), tq)
        result = jnp.where(out_lane, new_accs[0], new_accs[1]).astype(o_ref.dtype)
        o_ref[0, out_rows, :] = jnp.where(ids[3] < n_items, result, o_ref[0, out_rows, :])
        return new_carries, new_accs

    def advance(slot, st):
        ptr, in2, in3, in4 = st[:4]
        carries, accs = [st[4], st[5]], [st[6], st[7]]
        in1 = jnp.minimum(ptr, n_items)
        carries, accs = step(slot, (in1, in2, in3, in4), carries, accs)
        nxt = jnp.minimum(ptr + 1, n_items)
        tile3 = qi_tab[in3]
        done = jnp.max(jnp.maximum(carries[0], carries[1])) < SB_EXIT
        jump = done & (in3 < n_items) & (nxt < n_items) & (qi_tab[nxt] == tile3)
        ptr = jnp.where(jump, ts_tab[tile3 + 1], nxt)
        return (ptr, in1, in2, in3, carries[0], carries[1], accs[0], accs[1])

    def body(st):
        return advance(1, advance(0, st))

    def pending(st):
        ptr, in2, in3, in4 = st[:4]
        return (ptr < n_items) | (in2 < n_items) | (in3 < n_items) | (in4 < n_items)

    zeros = jnp.zeros((tq, LANES), F32)
    filler = jnp.int32(n_items)
    lax.while_loop(pending, body, (jnp.int32(0), filler, filler, filler, zeros, zeros, zeros, zeros))


def sb_attention(proj, q_gain, k_gain, *, batch, seq, tq, tk):
    tq = min(tq, seq)
    tk = min(tk, tq)
    assert seq % tq == 0 and tq % tk == 0
    n_pairs = SB_HEADS // 2
    n_items, qi, kj, tile_start = _sb_items(seq // tq, tq // tk)
    g2 = lambda g: jnp.concatenate([g, g]).reshape(1, LANES).astype(F32)
    seq_blk = lambda s0: pl.BlockSpec((1, seq, LANES), lambda b, p, qt, kt, ts: (s0 + p, b, 0))
    stage = lambda dt: pltpu.VMEM((2, 2, tq, tk), dt)
    return pl.pallas_call(
        functools.partial(_sb_kernel, tq=tq, tk=tk, seq=seq, n_items=n_items),
        grid_spec=pltpu.PrefetchScalarGridSpec(
            num_scalar_prefetch=3,
            grid=(batch, n_pairs),
            in_specs=[
                seq_blk(SLAB_SB_Q), seq_blk(SLAB_SB_K), seq_blk(SLAB_SB_V),
                pl.BlockSpec((1, LANES), lambda b, p, qt, kt, ts: (0, 0)),
                pl.BlockSpec((1, LANES), lambda b, p, qt, kt, ts: (0, 0)),
            ],
            out_specs=pl.BlockSpec((1, seq, LANES), lambda b, p, qt, kt, ts: (p, b, 0)),
            scratch_shapes=[
                pltpu.VMEM((2, seq, LANES), BF16),
                pltpu.VMEM((seq, LANES), BF16),
                stage(F32), stage(BF16), stage(BF16), stage(BF16),
                pltpu.VMEM((2, 2, tq, LANES), F32),
            ],
        ),
        out_shape=jax.ShapeDtypeStruct((n_pairs, batch * seq, LANES), BF16),
        compiler_params=_params("parallel", "parallel"),
        name="sb_attention",
    )(jnp.asarray(qi, jnp.int32), jnp.asarray(kj, jnp.int32), jnp.asarray(tile_start, jnp.int32), proj, proj, proj, g2(q_gain), g2(k_gain))


GLA_ROWS = 256
GLA_KW = GLA_HEADS * GLA_DK


def _gla_kernel(q_ref, k_ref, v_ref, a_ref, r_ref, aup_ref, ab_ref, og_ref, o_ref, state_ref):
    rows = GLA_ROWS
    n_chunks = rows // GLA_CHUNK

    @pl.when(pl.program_id(1) == 0)
    def _():
        state_ref[...] = jnp.zeros_like(state_ref)

    pre = jnp.dot(a_ref[0], aup_ref[...], preferred_element_type=F32) + ab_ref[...]
    log_a = -_softplus(-pre) * (1.0 / GLA_GATE_TAU)
    la_hi, la_lo = _split_bf16(log_a)

    ri = lax.broadcasted_iota(jnp.int32, (rows, rows), 0)
    ci = lax.broadcasted_iota(jnp.int32, (rows, rows), 1)
    same_chunk = (ri // GLA_CHUNK) == (ci // GLA_CHUNK)
    tri = same_chunk & (ri >= ci)
    tri_b = jnp.where(tri, 1.0, 0.0).astype(BF16)
    ones_b = jnp.where(same_chunk, 1.0, 0.0).astype(BF16)
    cum = (jnp.dot(tri_b, la_hi, preferred_element_type=F32)
           + jnp.dot(tri_b, la_lo, preferred_element_type=F32))
    cum_last = (jnp.dot(ones_b, la_hi, preferred_element_type=F32)
                + jnp.dot(ones_b, la_lo, preferred_element_type=F32))

    q_all = jnp.concatenate([q_ref[0], q_ref[1]], axis=-1).astype(F32)
    k_all = jnp.concatenate([k_ref[0], k_ref[1]], axis=-1).astype(F32)
    q_dec = q_all * (GLA_DK ** -0.5) * jnp.exp(cum)
    k_inv = (k_all * jnp.exp(-cum)).astype(BF16)
    k_end = (k_all * jnp.exp(cum_last - cum)).astype(BF16)
    v_all = jnp.concatenate([v_ref[h] for h in range(GLA_HEADS)], axis=-1)

    lane = lax.broadcasted_iota(jnp.int32, (rows, GLA_KW), 1)
    q_heads = [jnp.where((lane // GLA_DK) == h, q_dec, 0.0).astype(BF16) for h in range(GLA_HEADS)]

    o_heads = []
    for h in range(GLA_HEADS):
        sc = lax.dot_general(q_heads[h], k_inv, (((1,), (1,)), ((), ())), preferred_element_type=F32)
        sc = jnp.where(tri, sc, 0.0).astype(BF16)
        o_heads.append(jnp.dot(sc, v_ref[h], preferred_element_type=F32))

    ones_cols = jnp.ones((GLA_CHUNK, LANES), BF16)
    state = state_ref[...]
    inter = [[] for _ in range(GLA_HEADS)]
    for c in range(n_chunks):
        r0, r1 = c * GLA_CHUNK, (c + 1) * GLA_CHUNK
        state_b = state.astype(BF16)
        for h in range(GLA_HEADS):
            inter[h].append(jnp.dot(q_heads[h][r0:r1], state_b, preferred_element_type=F32))
        kv = lax.dot_general(k_end[r0:r1], v_all[r0:r1], (((0,), (0,)), ((), ())),
                             preferred_element_type=F32)
        kv_sel = jnp.concatenate(
            [kv[h * GLA_DK:(h + 1) * GLA_DK, h * GLA_DV:(h + 1) * GLA_DV] for h in range(GLA_HEADS)], axis=0)
        tot = (lax.dot_general(la_hi[r0:r1], ones_cols, (((0,), (0,)), ((), ())), preferred_element_type=F32)
               + lax.dot_general(la_lo[r0:r1], ones_cols, (((0,), (0,)), ((), ())), preferred_element_type=F32))
        state = jnp.exp(tot) * state + kv_sel
    state_ref[...] = state

    for h in range(GLA_HEADS):
        o = o_heads[h] + jnp.concatenate(inter[h], axis=0)
        o = _rms_rows(o) * og_ref[...]
        r = r_ref[h].astype(F32)
        o_ref[h] = (o * (r * jax.nn.sigmoid(r))).astype(o_ref.dtype)


def gla(proj, a_up, a_b, o_gain, *, batch, seq):
    rows = GLA_ROWS
    assert seq % rows == 0
    nb = seq // rows
    a_up_p = jnp.zeros((LANES, GLA_KW), BF16).at[:GLA_GATE_RANK].set(a_up.astype(BF16))
    row_blk = lambda n, s0: pl.BlockSpec((n, rows, LANES), lambda b, i: (s0 // n, b * nb + i, 0))
    return pl.pallas_call(
        _gla_kernel,
        grid=(batch, nb),
        in_specs=[
            row_blk(2, SLAB_GLA_Q), row_blk(2, SLAB_GLA_K), row_blk(4, SLAB_GLA_V),
            row_blk(1, SLAB_GLA_A), row_blk(4, SLAB_GLA_R),
            pl.BlockSpec((LANES, GLA_KW), lambda b, i: (0, 0)),
            pl.BlockSpec((1, GLA_KW), lambda b, i: (0, 0)),
            pl.BlockSpec((1, GLA_DV), lambda b, i: (0, 0)),
        ],
        out_specs=pl.BlockSpec((GLA_HEADS, rows, LANES), lambda b, i: (0, b * nb + i, 0)),
        out_shape=jax.ShapeDtypeStruct((GLA_HEADS, batch * seq, LANES), BF16),
        scratch_shapes=[pltpu.VMEM((GLA_KW, GLA_DV), F32)],
        compiler_params=_params("parallel", "arbitrary"),
        name="gla",
    )(proj, proj, proj, proj, proj, a_up_p,
      a_b.reshape(1, GLA_KW).astype(F32), o_gain.reshape(1, GLA_DV).astype(F32))


def _merge_kernel(x_ref, oa_ref, ob_ref, cq_ref, gate_ref, mkv_ref, cqg_ref, ckg_ref,
                  wsb_ref, wgla_ref, wca_ref, wout_ref, o_ref):
    ca_w = CA_HEADS * CA_HEAD_DIM
    oc = []
    for h in range(CA_HEADS):
        q = _rms_rows(cq_ref[h].astype(F32)) * (cqg_ref[...] * CA_HEAD_DIM ** -0.5)
        kk = _rms_rows(mkv_ref[:, h * CA_HEAD_DIM:(h + 1) * CA_HEAD_DIM].astype(F32)) * ckg_ref[...]
        s = lax.dot_general(q.astype(BF16), kk.astype(BF16), (((1,), (1,)), ((), ())),
                            preferred_element_type=F32)
        e = jnp.exp(s - jnp.max(s, axis=-1, keepdims=True))
        p = e * (1.0 / jnp.sum(e, axis=-1, keepdims=True))
        vv = mkv_ref[:, ca_w + h * CA_HEAD_DIM: ca_w + (h + 1) * CA_HEAD_DIM]
        oc.append(jnp.dot(p.astype(BF16), vv, preferred_element_type=F32).astype(BF16))
    o_c = jnp.concatenate(oc, axis=-1)
    o_a = jnp.concatenate([oa_ref[s] for s in range(4)], axis=-1)
    o_b = jnp.concatenate([ob_ref[s] for s in range(4)], axis=-1)

    def gate(b):
        g = jnp.concatenate([gate_ref[b * 8 + s] for s in range(8)], axis=-1).astype(F32)
        return jax.nn.sigmoid(g)

    merged = gate(0) * jnp.dot(o_a, wsb_ref[...], preferred_element_type=F32)
    merged += gate(1) * jnp.dot(o_b, wgla_ref[...], preferred_element_type=F32)
    merged += gate(2) * jnp.dot(o_c, wca_ref[...], preferred_element_type=F32)
    o_ref[...] = x_ref[...] + jnp.dot(merged.astype(BF16), wout_ref[...], preferred_element_type=F32)


def merge_branches(x, o_a, o_b, proj, mkv, cq_gain, ck_gain, w_sb, w_gla, w_ca, w_out, *, batch, seq, tm):
    n, d = x.shape
    tm = min(tm, seq)
    assert seq % tm == 0
    nt = seq // tm
    mem_len = mkv.shape[0] // batch
    full = lambda a: pl.BlockSpec(a.shape, lambda i: (0,) * a.ndim)
    cqg = cq_gain.reshape(1, CA_HEAD_DIM).astype(F32)
    ckg = ck_gain.reshape(1, CA_HEAD_DIM).astype(F32)
    return pl.pallas_call(
        _merge_kernel,
        grid=(n // tm,),
        in_specs=[
            pl.BlockSpec((tm, d), lambda i: (i, 0)),
            pl.BlockSpec((4, tm, LANES), lambda i: (0, i, 0)),
            pl.BlockSpec((4, tm, LANES), lambda i: (0, i, 0)),
            pl.BlockSpec((4, tm, LANES), lambda i: (SLAB_CA_Q // 4, i, 0)),
            pl.BlockSpec((24, tm, LANES), lambda i: (SLAB_GATES // 24, i, 0)),
            pl.BlockSpec((mem_len, mkv.shape[1]), lambda i: (i // nt, 0)),
            full(cqg), full(ckg), full(w_sb), full(w_gla), full(w_ca), full(w_out),
        ],
        out_specs=pl.BlockSpec((tm, d), lambda i: (i, 0)),
        out_shape=jax.ShapeDtypeStruct((n, d), F32),
        compiler_params=_params("parallel"),
        name="merge_branches",
    )(x, o_a, o_b, proj, proj, mkv,
      cqg, ckg, w_sb, w_gla, w_ca, w_out)


def _ffn_kernel(x_ref, g_ref, wg_ref, wu_ref, wd_ref, o_ref, hn_ref, acc_ref):
    f = pl.program_id(1)

    @pl.when(f == 0)
    def _():
        hn_ref[...] = (_rms_rows(x_ref[...]) * g_ref[...]).astype(BF16)
        acc_ref[...] = jnp.zeros_like(acc_ref)

    h = hn_ref[...]
    gt = jnp.dot(h, wg_ref[...], preferred_element_type=F32)
    up = jnp.dot(h, wu_ref[...], preferred_element_type=F32)
    act = (gt * jax.nn.sigmoid(gt) * up).astype(BF16)
    acc_ref[...] += jnp.dot(act, wd_ref[...], preferred_element_type=F32)

    @pl.when(f == pl.num_programs(1) - 1)
    def _():
        o_ref[...] = x_ref[...] + acc_ref[...]


def ffn_dense(x, g, w_gate_up, w_down, *, tm, tf):
    n, d = x.shape
    d_ff = w_down.shape[0]
    tm = min(tm, n)
    assert n % tm == 0 and d_ff % tf == 0 and tf % LANES == 0
    nf = d_ff // tf
    return pl.pallas_call(
        _ffn_kernel,
        grid=(n // tm, nf),
        in_specs=[
            pl.BlockSpec((tm, d), lambda i, f: (i, 0)),
            pl.BlockSpec((1, d), lambda i, f: (0, 0)),
            pl.BlockSpec((d, tf), lambda i, f: (0, f)),
            pl.BlockSpec((d, tf), lambda i, f: (0, nf + f)),
            pl.BlockSpec((tf, d), lambda i, f: (f, 0)),
        ],
        out_specs=pl.BlockSpec((tm, d), lambda i, f: (i, 0)),
        out_shape=jax.ShapeDtypeStruct((n, d), F32),
        scratch_shapes=[pltpu.VMEM((tm, d), BF16), pltpu.VMEM((tm, d), F32)],
        compiler_params=_params("parallel", "arbitrary"),
        name="ffn_dense",
    )(x, g.reshape(1, d), w_gate_up, w_gate_up, w_down)


def _router_kernel(x_ref, g_ref, wr_ref, c_ref):
    hn = _rms_rows(x_ref[...]) * g_ref[...]
    logits = jnp.dot(hn, wr_ref[...], preferred_element_type=F32, precision=lax.Precision.HIGHEST)
    lane = lax.broadcasted_iota(jnp.int32, logits.shape, 1).astype(F32)
    neg = jnp.float32(-jnp.inf)
    logits = jnp.where(lane < N_EXPERTS, logits, neg)
    m1 = jnp.max(logits, axis=-1, keepdims=True)
    i1 = jnp.min(jnp.where(logits == m1, lane, float(LANES)), axis=-1, keepdims=True)
    rest = jnp.where(lane == i1, neg, logits)
    m2 = jnp.max(rest, axis=-1, keepdims=True)
    i2 = jnp.min(jnp.where(rest == m2, lane, float(LANES)), axis=-1, keepdims=True)
    e2 = jnp.exp(m2 - m1)
    inv = 1.0 / (1.0 + e2)
    c_ref[...] = (jnp.where(lane == 0.0, inv, 0.0) + jnp.where(lane == 1.0, e2 * inv, 0.0)
                  + jnp.where(lane == 2.0, i1, 0.0) + jnp.where(lane == 3.0, i2, 0.0))


def router(x, g, w_router, *, tm):
    n, d = x.shape
    tm = min(tm, n)
    wr = jnp.zeros((d, LANES), F32).at[:, :N_EXPERTS].set(w_router)
    return pl.pallas_call(
        _router_kernel,
        grid=(n // tm,),
        in_specs=[
            pl.BlockSpec((tm, d), lambda i: (i, 0)),
            pl.BlockSpec((1, d), lambda i: (0, 0)),
            pl.BlockSpec((d, LANES), lambda i: (0, 0)),
        ],
        out_specs=pl.BlockSpec((tm, LANES), lambda i: (i, 0)),
        out_shape=jax.ShapeDtypeStruct((n, LANES), F32),
        compiler_params=_params("parallel"),
        name="router",
    )(x, g.reshape(1, d), wr)


def _row_copy(src_hbm, row, dst_vmem, r, sem):
    return pltpu.make_async_copy(src_hbm.at[pl.ds(row, 1)], dst_vmem.at[pl.ds(r, 1)], sem)


def _expert_ffn_kernel(te_ref, nu_ref, cur_ref, nxt_ref, x_hbm, g_ref, wg_ref, wu_ref, wd_ref, o_ref,
                       xbuf_ref, sem, hn_ref, acc_ref):
    i = pl.program_id(0)
    f = pl.program_id(1)
    rows = o_ref.shape[0]
    used = i < nu_ref[0]
    slot = i % 2

    def issue(idx_ref, s):
        def one(r, carry):
            _row_copy(x_hbm, idx_ref[0, 0, r], xbuf_ref.at[s], r, sem.at[s]).start()
            return carry
        lax.fori_loop(0, rows, one, 0, unroll=GATHER_UNROLL)

    @pl.when((i == 0) & (f == 0))
    def _():
        issue(cur_ref, 0)

    @pl.when(used & (f == 0))
    def _():
        def one(r, carry):
            _row_copy(x_hbm, 0, xbuf_ref.at[slot], r, sem.at[slot]).wait()
            return carry
        lax.fori_loop(0, rows, one, 0, unroll=GATHER_UNROLL)

        @pl.when(i + 1 < nu_ref[0])
        def _():
            issue(nxt_ref, 1 - slot)

        hn_ref[...] = (_rms_rows(xbuf_ref[slot]) * g_ref[...]).astype(BF16)
        acc_ref[...] = jnp.zeros_like(acc_ref)

    @pl.when(used)
    def _():
        h = hn_ref[...]
        gt = jnp.dot(h, wg_ref[0], preferred_element_type=F32)
        up = jnp.dot(h, wu_ref[0], preferred_element_type=F32)
        act = (gt * jax.nn.sigmoid(gt) * up).astype(BF16)
        acc_ref[...] += jnp.dot(act, wd_ref[0], preferred_element_type=F32)

    @pl.when(f == pl.num_programs(1) - 1)
    def _():
        o_ref[...] = jnp.where(used, acc_ref[...], 0.0)


def expert_ffn(x, src, g, tile_expert, n_used, w_gate_up, w_down, *, tm, tf):
    rows = src.shape[0]
    d = x.shape[1]
    d_ff = w_down.shape[1]
    assert rows % tm == 0 and d_ff % tf == 0 and tf % LANES == 0
    nf = d_ff // tf
    n_tiles = rows // tm
    src = src.reshape(n_tiles, 1, tm)
    return pl.pallas_call(
        _expert_ffn_kernel,
        grid_spec=pltpu.PrefetchScalarGridSpec(
            num_scalar_prefetch=2,
            grid=(n_tiles, nf),
            in_specs=[
                pl.BlockSpec((1, 1, tm), lambda i, f, te, nu: (i, 0, 0), memory_space=pltpu.SMEM),
                pl.BlockSpec((1, 1, tm), lambda i, f, te, nu: (jnp.minimum(i + 1, n_tiles - 1), 0, 0),
                             memory_space=pltpu.SMEM),
                pl.BlockSpec(memory_space=pl.ANY),
                pl.BlockSpec((1, d), lambda i, f, te, nu: (0, 0)),
                pl.BlockSpec((1, d, tf), lambda i, f, te, nu: (te[i], 0, f)),
                pl.BlockSpec((1, d, tf), lambda i, f, te, nu: (te[i], 0, nf + f)),
                pl.BlockSpec((1, tf, d), lambda i, f, te, nu: (te[i], f, 0)),
            ],
            out_specs=pl.BlockSpec((tm, d), lambda i, f, te, nu: (i, 0)),
            scratch_shapes=[pltpu.VMEM((2, tm, d), F32), pltpu.SemaphoreType.DMA((2,)),
                            pltpu.VMEM((tm, d), BF16), pltpu.VMEM((tm, d), F32)],
        ),
        out_shape=jax.ShapeDtypeStruct((rows, d), F32),
        compiler_params=_params("arbitrary", "arbitrary"),
        name="expert_ffn",
    )(tile_expert, n_used, src, src, x, g.reshape(1, d), w_gate_up, w_gate_up, w_down)


def _combine_kernel(p1_ref, p2_ref, n1_ref, n2_ref, x_ref, rw_ref, ys_hbm, o_ref, a_ref, b_ref, sem):
    i = pl.program_id(0)
    rows = o_ref.shape[0]
    slot = i % 2

    def issue(q1_ref, q2_ref, s):
        def one(r, carry):
            _row_copy(ys_hbm, q1_ref[0, 0, r], a_ref.at[s], r, sem.at[0, s]).start()
            _row_copy(ys_hbm, q2_ref[0, 0, r], b_ref.at[s], r, sem.at[1, s]).start()
            return carry
        lax.fori_loop(0, rows, one, 0, unroll=GATHER_UNROLL)

    @pl.when(i == 0)
    def _():
        issue(p1_ref, p2_ref, 0)

    @pl.when(i + 1 < pl.num_programs(0))
    def _():
        issue(n1_ref, n2_ref, 1 - slot)

    def drain(r, carry):
        _row_copy(ys_hbm, 0, a_ref.at[slot], r, sem.at[0, slot]).wait()
        _row_copy(ys_hbm, 0, b_ref.at[slot], r, sem.at[1, slot]).wait()
        return carry
    lax.fori_loop(0, rows, drain, 0, unroll=GATHER_UNROLL)

    rw = rw_ref[...]
    o_ref[...] = x_ref[...] + (rw[:, 0:1] * a_ref[slot] + rw[:, 1:2] * b_ref[slot])


def moe_combine(x, route, ys, pos1, pos2, *, tc):
    n, d = x.shape
    assert n % tc == 0
    nt = n // tc
    idx_spec = pl.BlockSpec((1, 1, tc), lambda i: (i, 0, 0), memory_space=pltpu.SMEM)
    nxt_spec = pl.BlockSpec((1, 1, tc), lambda i: (jnp.minimum(i + 1, nt - 1), 0, 0), memory_space=pltpu.SMEM)
    pos1 = pos1.reshape(nt, 1, tc)
    pos2 = pos2.reshape(nt, 1, tc)
    return pl.pallas_call(
        _combine_kernel,
        grid=(nt,),
        in_specs=[
            idx_spec, idx_spec, nxt_spec, nxt_spec,
            pl.BlockSpec((tc, d), lambda i: (i, 0)),
            pl.BlockSpec((tc, LANES), lambda i: (i, 0)),
            pl.BlockSpec(memory_space=pl.ANY),
        ],
        out_specs=pl.BlockSpec((tc, d), lambda i: (i, 0)),
        out_shape=jax.ShapeDtypeStruct((n, d), F32),
        scratch_shapes=[pltpu.VMEM((2, tc, d), F32), pltpu.VMEM((2, tc, d), F32),
                        pltpu.SemaphoreType.DMA((2, 2))],
        compiler_params=_params("arbitrary"),
        name="moe_combine",
    )(pos1, pos2, pos1, pos2, x, route, ys)


def _dispatch_plan(route, *, tm):
    n = route.shape[0]
    ids = route[:, 2:4].astype(jnp.int32)
    e_flat = ids.T.reshape(-1)
    onehot = (e_flat[:, None] == jnp.arange(N_EXPERTS)[None, :]).astype(jnp.int32)
    running = jnp.cumsum(onehot, axis=0)
    rank = jnp.sum(running * onehot, axis=1) - 1
    counts = running[-1]
    padded = ((counts + tm - 1) // tm) * tm
    ends = jnp.cumsum(padded)
    starts = ends - padded
    pos = starts[e_flat] + rank
    rows = TOP_K * n + N_EXPERTS * tm
    tok = jnp.tile(jnp.arange(n, dtype=jnp.int32), TOP_K)
    src = jnp.zeros((rows,), jnp.int32).at[pos].set(tok)
    tile_start = jnp.arange(rows // tm, dtype=jnp.int32) * tm
    tile_expert = jnp.minimum(jnp.sum(tile_start[:, None] >= ends[None, :], axis=1), N_EXPERTS - 1)
    n_used = (ends[-1] // tm).reshape(1).astype(jnp.int32)
    return src, tile_expert.astype(jnp.int32), n_used, pos[:n].astype(jnp.int32), pos[n:].astype(jnp.int32)


def ffn_moe(x, g, route, w_gate_up, w_down, *, tm, tf):
    src, tile_expert, n_used, pos1, pos2 = _dispatch_plan(route, tm=tm)
    ys = expert_ffn(x, src, g, tile_expert, n_used, w_gate_up, w_down, tm=tm, tf=tf)
    return moe_combine(x, route, ys, pos1, pos2, tc=min(256, x.shape[0]))


def _slab_weights(w_in):
    d = w_in.shape[0]
    sbw = SB_HEADS * SB_HEAD_DIM
    glv = GLA_HEADS * GLA_DV
    caw = CA_HEADS * CA_HEAD_DIM
    sizes = (sbw, sbw, sbw, GLA_KW, GLA_KW, glv, GLA_GATE_RANK, glv, caw, N_BRANCH * D_MODEL)
    parts, off = [], 0
    for s in sizes:
        parts.append(w_in[:, off:off + s])
        off += s
    sb_q, sb_k, sb_v, g_q, g_k, g_v, g_a, g_r, ca_q, gates = parts
    pad = lambda cols: jnp.zeros((d, cols), w_in.dtype)
    cols = [sb_q, sb_k, sb_v, g_v, g_r, ca_q, gates, g_q, g_k, g_a, pad(2 * LANES - GLA_GATE_RANK)]
    w = jnp.concatenate(cols, axis=1).astype(BF16)
    assert w.shape[1] == N_SLABS * LANES
    return w


def kernel(x, mem, mix_norm_g, w_in, sb_q_g, sb_k_g, gla_a_up, gla_a_b, gla_o_g, ca_q_g, ca_k_g, mem_norm_g, w_mem_kv, w_br_sb, w_br_gla, w_br_ca, w_out, ffn_norm_g, ff_w_gate_up, ff_w_down, moe_w_router, moe_w_gate_up, moe_w_down):
    batch, seq, d = x.shape
    depth = w_in.shape[0]
    n = batch * seq
    xf = x.reshape(n, d)
    memf = mem.reshape(-1, d)
    bf = lambda a: a.astype(BF16)

    for layer in range(depth):
        proj = norm_matmul(xf, mix_norm_g[layer], _slab_weights(w_in[layer]), tm=512, tn=2304, slabs=True)
        mkv = norm_matmul(memf, mem_norm_g[layer], bf(w_mem_kv[layer]), tm=512, tn=1024, slabs=False)
        o_a = sb_attention(proj, sb_q_g[layer], sb_k_g[layer], batch=batch, seq=seq, tq=256, tk=128)
        o_b = gla(proj, gla_a_up[layer], gla_a_b[layer], gla_o_g[layer], batch=batch, seq=seq)
        xf = merge_branches(xf, o_a, o_b, proj, mkv, ca_q_g[layer], ca_k_g[layer],
                            bf(w_br_sb[layer]), bf(w_br_gla[layer]), bf(w_br_ca[layer]), bf(w_out[layer]),
                            batch=batch, seq=seq, tm=512)
        j = layer // 2
        if layer % 2 == 0:
            xf = ffn_dense(xf, ffn_norm_g[layer], bf(ff_w_gate_up[j]), bf(ff_w_down[j]), tm=512, tf=1408)
        else:
            route = router(xf, ffn_norm_g[layer], moe_w_router[j], tm=512)
            xf = ffn_moe(xf, ffn_norm_g[layer], route, bf(moe_w_gate_up[j]), bf(moe_w_down[j]), tm=512, tf=896)
    return xf.reshape(batch, seq, d)
```

```python
import functools

import jax
import jax.numpy as jnp
from jax import lax
from jax.experimental import pallas as pl
from jax.experimental.pallas import tpu as pltpu

F32 = jnp.float32
BF16 = jnp.bfloat16

LANES = 128
EPS = 1e-6
LOG2E = 1.4426950408889634
LN2 = 0.6931471805599453
MASKED = 1e30
GATHER_UNROLL = 8
SB_STAGES = 4
SB_EXIT = -110.0
VMEM_LIMIT_BYTES = 56 * 1024 * 1024

D_MODEL = 1024
SB_HEADS, SB_HEAD_DIM = 8, 64
GLA_HEADS, GLA_DK, GLA_DV = 4, 64, 128
GLA_GATE_RANK, GLA_GATE_TAU, GLA_CHUNK = 16, 16.0, 64
CA_HEADS, CA_HEAD_DIM = 4, 128
N_BRANCH = 3
N_EXPERTS, TOP_K = 8, 2

SLAB_SB_Q, SLAB_SB_K, SLAB_SB_V = 0, 4, 8
SLAB_GLA_V, SLAB_GLA_R = 12, 16
SLAB_CA_Q = 20
SLAB_GATES = 24
SLAB_GLA_Q, SLAB_GLA_K, SLAB_GLA_A = 48, 50, 52
N_SLABS = 54


def _params(*semantics):
    return pltpu.CompilerParams(dimension_semantics=semantics, vmem_limit_bytes=VMEM_LIMIT_BYTES)


def _rms_rows(x, eps=EPS):
    return x * lax.rsqrt(jnp.mean(x * x, axis=-1, keepdims=True) + eps)


def _rms_half_lanes(x, eps=EPS):
    lane = lax.broadcasted_iota(jnp.int32, x.shape, 1)
    lo = lane < SB_HEAD_DIM
    sq = x * x
    s_lo = jnp.sum(jnp.where(lo, sq, 0.0), axis=-1, keepdims=True)
    s_hi = jnp.sum(jnp.where(lo, 0.0, sq), axis=-1, keepdims=True)
    ms = jnp.where(lo, s_lo, s_hi) * (1.0 / SB_HEAD_DIM)
    return x * lax.rsqrt(ms + eps)


def _split_bf16(x):
    hi = x.astype(BF16)
    lo = (x - hi.astype(F32)).astype(BF16)
    return hi, lo


def _norm_matmul_kernel(x_ref, g_ref, w_ref, o_ref, xn_ref, *, n_slab):
    @pl.when(pl.program_id(1) == 0)
    def _():
        xn_ref[...] = (_rms_rows(x_ref[...]) * g_ref[...]).astype(BF16)

    acc = jnp.dot(xn_ref[...], w_ref[...], preferred_element_type=F32)
    if n_slab is None:
        o_ref[...] = acc.astype(o_ref.dtype)
    else:
        for s in range(n_slab):
            o_ref[s] = acc[:, s * LANES:(s + 1) * LANES].astype(o_ref.dtype)


def norm_matmul(x, g, w, *, tm, tn, slabs):
    n, k = x.shape
    m = w.shape[1]
    tm = min(tm, n)
    tn = min(tn, m)
    assert n % tm == 0 and m % tn == 0 and tn % LANES == 0
    if slabs:
        n_slab = tn // LANES
        out_shape = jax.ShapeDtypeStruct((m // LANES, n, LANES), BF16)
        out_spec = pl.BlockSpec((n_slab, tm, LANES), lambda i, j: (j, i, 0))
    else:
        n_slab = None
        out_shape = jax.ShapeDtypeStruct((n, m), BF16)
        out_spec = pl.BlockSpec((tm, tn), lambda i, j: (i, j))
    return pl.pallas_call(
        functools.partial(_norm_matmul_kernel, n_slab=n_slab),
        grid=(n // tm, m // tn),
        in_specs=[
            pl.BlockSpec((tm, k), lambda i, j: (i, 0)),
            pl.BlockSpec((1, k), lambda i, j: (0, 0)),
            pl.BlockSpec((k, tn), lambda i, j: (0, j)),
        ],
        out_specs=out_spec,
        out_shape=out_shape,
        scratch_shapes=[pltpu.VMEM((tm, k), BF16)],
        compiler_params=_params("parallel", "arbitrary"),
        name="norm_matmul",
    )(x, g.reshape(1, k), w)


def _softplus(z):
    return jnp.maximum(z, 0.0) + jnp.log(1.0 + jnp.exp(-jnp.abs(z)))


def _sb_items(nq, n_diag):
    qi, kj = [], []
    for i in range(nq):
        for j in reversed(range((i + 1) * n_diag)):
            qi.append(i)
            kj.append(j)
    n_items = len(qi)
    tile_start = [i * (i + 1) // 2 * n_diag for i in range(nq + 1)]
    assert tile_start[nq] == n_items
    qi += [0] * n_diag * SB_STAGES
    kj += list(reversed(range(n_diag))) * SB_STAGES
    return n_items, qi, kj, tile_start


def _sb_kernel(qi_tab, kj_tab, ts_tab, q_ref, k_ref, v_ref, qg_ref, kg_ref, o_ref,
               qh_ref, kn_ref, z_ref, sp_ref, bt_ref, w_ref, rs_ref, *, tq, tk, seq, n_items):
    n_diag = tq // tk
    rows = min(512, seq)
    lane = lax.broadcasted_iota(jnp.int32, (rows, LANES), 1)
    scale = SB_HEAD_DIM ** -0.5

    def norm_chunk(c, carry):
        r0 = pl.multiple_of(c * rows, rows)
        kk = k_ref[0, pl.ds(r0, rows), :].astype(F32)
        kn_ref[pl.ds(r0, rows), :] = (_rms_half_lanes(kk) * kg_ref[...]).astype(BF16)
        qn = _rms_half_lanes(q_ref[0, pl.ds(r0, rows), :].astype(F32)) * (qg_ref[...] * scale)
        qh_ref[0, pl.ds(r0, rows), :] = jnp.where(lane < SB_HEAD_DIM, qn, 0.0).astype(BF16)
        qh_ref[1, pl.ds(r0, rows), :] = jnp.where(lane >= SB_HEAD_DIM, qn, 0.0).astype(BF16)
        return carry
    lax.fori_loop(0, seq // rows, norm_chunk, 0)

    z_ref[...] = jnp.zeros_like(z_ref)
    sp_ref[...] = jnp.zeros_like(sp_ref)
    bt_ref[...] = jnp.zeros_like(bt_ref)
    w_ref[...] = jnp.zeros_like(w_ref)
    rs_ref[...] = jnp.zeros_like(rs_ref)
    o_ref[...] = jnp.zeros_like(o_ref)

    rj = lax.broadcasted_iota(jnp.int32, (tk, tk), 0)
    cs = lax.broadcasted_iota(jnp.int32, (tk, tk), 1)
    u_strict = jnp.where(rj > cs, -1.0, 0.0).astype(BF16)
    rel = (lax.broadcasted_iota(jnp.int32, (tq, tk), 1) - lax.broadcasted_iota(jnp.int32, (tq, tk), 0))
    out_lane = lax.broadcasted_iota(jnp.int32, (tq, LANES), 1) < SB_HEAD_DIM

    def item(m):
        return qi_tab[m], kj_tab[m]

    def step(slot, ids, carries, accs):
        prev = 1 - slot
        qi1, kj1 = item(ids[0])
        kb = kn_ref[pl.ds(pl.multiple_of(kj1 * tk, tk), tk), :]
        for h in range(2):
            qh = qh_ref[h, pl.ds(pl.multiple_of(qi1 * tq, tq), tq), :]
            z_ref[slot, h] = lax.dot_general(qh, kb, (((1,), (1,)), ((), ())), preferred_element_type=F32)
        qi2, kj2 = item(ids[1])
        keep = rel < (qi2 * tq - kj2 * tk)
        for h in range(2):
            z = jnp.where(keep, z_ref[prev, h], -MASKED)
            e = jnp.exp2(jnp.abs(z) * (-LOG2E))
            r = pl.reciprocal(1.0 + e)
            sp =jnp.maximum(z, 0.0) - jnp.log(r)
            beta = jnp.where(z >= 0.0, r, e * r)
            sp_ref[slot, h] = sp.astype(BF16)
            bt_ref[slot, h] = beta.astype(BF16)
            rs_ref[slot, h] = jnp.broadcast_to(jnp.sum(sp, axis=1, keepdims=True), (tq, LANES))
        qi3, kj3 = item(ids[2])
        first3 = kj3 == (qi3 + 1) * n_diag - 1
        new_carries = []
        for h in range(2):
            t = jnp.dot(sp_ref[prev, h], u_strict, preferred_element_type=F32)
            c_in = jnp.where(first3, 0.0, carries[h])
            c_wide = jnp.concatenate([c_in] * (tk // LANES), axis=1)
            w = jnp.exp2((t + c_wide) * LOG2E) * bt_ref[prev, h].astype(F32)
            w_ref[slot, h] = w.astype(BF16)
            new_carries.append(c_in - rs_ref[prev, h])
        qi4, kj4 = item(ids[3])
        first4 = kj4 == (qi4 + 1) * n_diag - 1
        vb = v_ref[0, pl.ds(pl.multiple_of(kj4 * tk, tk), tk), :]
        new_accs = []
        for h in range(2):
            pv = jnp.dot(w_ref[prev, h], vb, preferred_element_type=F32)
            new_accs.append(jnp.where(first4, 0.0, accs[h]) + pv)
        out_rows = pl.ds(pl.multiple_of(qi4 * tq, tq), tq)
        result = jnp.where(out_lane, new_accs[0], new_accs[1]).astype(o_ref.dtype)
        o_ref[0, out_rows, :] = jnp.where(ids[3] < n_items, result, o_ref[0, out_rows, :])
        return new_carries, new_accs

    def advance(slot, st):
        ptr, in2, in3, in4 = st[:4]
        carries, accs = [st[4], st[5]], [st[6], st[7]]
        in1 = jnp.minimum(ptr, n_items)
        carries, accs = step(slot, (in1, in2, in3, in4), carries, accs)
        nxt = jnp.minimum(ptr + 1, n_items)
        tile3 = qi_tab[in3]
        done = jnp.max(jnp.maximum(carries[0], carries[1])) < SB_EXIT
        jump = done & (in3 < n_items) & (nxt < n_items) & (qi_tab[nxt] == tile3)
        ptr = jnp.where(jump, ts_tab[tile3 + 1], nxt)
        return (ptr, in1, in2, in3, carries[0], carries[1], accs[0], accs[1])

    def body(st):
        return advance(1, advance(0, st))

    def pending(st):
        ptr, in2, in3, in4 = st[:4]
        return (ptr < n_items) | (in2 < n_items) | (in3 < n_items) | (in4 < n_items)

    zeros = jnp.zeros((tq, LANES), F32)
    filler = jnp.int32(n_items)
    lax.while_loop(pending, body, (jnp.int32(0), filler, filler, filler, zeros, zeros, zeros, zeros))


def sb_attention(proj, q_gain, k_gain, *, batch, seq, tq, tk):
    tq = min(tq, seq)
    tk = min(tk, tq)
    assert seq % tq == 0 and tq % tk == 0
    n_pairs = SB_HEADS // 2
    n_items, qi, kj, tile_start = _sb_items(seq // tq, tq // tk)
    g2 = lambda g: jnp.concatenate([g, g]).reshape(1, LANES).astype(F32)
    seq_blk = lambda s0: pl.BlockSpec((1, seq, LANES), lambda b, p, qt, kt, ts: (s0 + p, b, 0))
    stage = lambda dt: pltpu.VMEM((2, 2, tq, tk), dt)
    return pl.pallas_call(
        functools.partial(_sb_kernel, tq=tq, tk=tk, seq=seq, n_items=n_items),
        grid_spec=pltpu.PrefetchScalarGridSpec(
            num_scalar_prefetch=3,
            grid=(batch, n_pairs),
            in_specs=[
                seq_blk(SLAB_SB_Q), seq_blk(SLAB_SB_K), seq_blk(SLAB_SB_V),
                pl.BlockSpec((1, LANES), lambda b, p, qt, kt, ts: (0, 0)),
                pl.BlockSpec((1, LANES), lambda b, p, qt, kt, ts: (0, 0)),
            ],
            out_specs=pl.BlockSpec((1, seq, LANES), lambda b, p, qt, kt, ts: (p, b, 0)),
            scratch_shapes=[
                pltpu.VMEM((2, seq, LANES), BF16),
                pltpu.VMEM((seq, LANES), BF16),
                stage(F32), stage(BF16), stage(BF16), stage(BF16),
                pltpu.VMEM((2, 2, tq, LANES), F32),
            ],
        ),
        out_shape=jax.ShapeDtypeStruct((n_pairs, batch * seq, LANES), BF16),
        compiler_params=_params("parallel", "parallel"),
        name="sb_attention",
    )(jnp.asarray(qi, jnp.int32), jnp.asarray(kj, jnp.int32), jnp.asarray(tile_start, jnp.int32), proj, proj, proj, g2(q_gain), g2(k_gain))


GLA_ROWS = 256
GLA_KW = GLA_HEADS * GLA_DK


def _gla_kernel(q_ref, k_ref, v_ref, a_ref, r_ref, aup_ref, ab_ref, og_ref, o_ref, state_ref):
    rows = GLA_ROWS
    n_chunks = rows // GLA_CHUNK

    @pl.when(pl.program_id(1) == 0)
    def _():
        state_ref[...] = jnp.zeros_like(state_ref)

    pre = jnp.dot(a_ref[0], aup_ref[...], preferred_element_type=F32) + ab_ref[...]
    log_a = -_softplus(-pre) * (1.0 / GLA_GATE_TAU)
    la_hi, la_lo = _split_bf16(log_a)

    ri = lax.broadcasted_iota(jnp.int32, (rows, rows), 0)
    ci = lax.broadcasted_iota(jnp.int32, (rows, rows), 1)
    same_chunk = (ri // GLA_CHUNK) == (ci // GLA_CHUNK)
    tri = same_chunk & (ri >= ci)
    tri_b = jnp.where(tri, 1.0, 0.0).astype(BF16)
    ones_b = jnp.where(same_chunk, 1.0, 0.0).astype(BF16)
    cum = (jnp.dot(tri_b, la_hi, preferred_element_type=F32)
           + jnp.dot(tri_b, la_lo, preferred_element_type=F32))
    cum_last = (jnp.dot(ones_b, la_hi, preferred_element_type=F32)
                + jnp.dot(ones_b, la_lo, preferred_element_type=F32))

    q_all = jnp.concatenate([q_ref[0], q_ref[1]], axis=-1).astype(F32)
    k_all = jnp.concatenate([k_ref[0], k_ref[1]], axis=-1).astype(F32)
    q_dec = q_all * (GLA_DK ** -0.5) * jnp.exp(cum)
    k_inv = (k_all * jnp.exp(-cum)).astype(BF16)
    k_end = (k_all * jnp.exp(cum_last - cum)).astype(BF16)
    v_all = jnp.concatenate([v_ref[h] for h in range(GLA_HEADS)], axis=-1)

    lane = lax.broadcasted_iota(jnp.int32, (rows, GLA_KW), 1)
    q_heads = [jnp.where((lane // GLA_DK) == h, q_dec, 0.0).astype(BF16) for h in range(GLA_HEADS)]

    o_heads = []
    for h in range(GLA_HEADS):
        sc = lax.dot_general(q_heads[h], k_inv, (((1,), (1,)), ((), ())), preferred_element_type=F32)
        sc = jnp.where(tri, sc, 0.0).astype(BF16)
        o_heads.append(jnp.dot(sc, v_ref[h], preferred_element_type=F32))

    ones_cols = jnp.ones((GLA_CHUNK, LANES), BF16)
    state = state_ref[...]
    inter = [[] for _ in range(GLA_HEADS)]
    for c in range(n_chunks):
        r0, r1 = c * GLA_CHUNK, (c + 1) * GLA_CHUNK
        state_b = state.astype(BF16)
        for h in range(GLA_HEADS):
            inter[h].append(jnp.dot(q_heads[h][r0:r1], state_b, preferred_element_type=F32))
        kv = lax.dot_general(k_end[r0:r1], v_all[r0:r1], (((0,), (0,)), ((), ())),
                             preferred_element_type=F32)
        kv_sel = jnp.concatenate(
            [kv[h * GLA_DK:(h + 1) * GLA_DK, h * GLA_DV:(h + 1) * GLA_DV] for h in range(GLA_HEADS)], axis=0)
        tot = (lax.dot_general(la_hi[r0:r1], ones_cols, (((0,), (0,)), ((), ())), preferred_element_type=F32)
               + lax.dot_general(la_lo[r0:r1], ones_cols, (((0,), (0,)), ((), ())), preferred_element_type=F32))
        state = jnp.exp(tot) * state + kv_sel
    state_ref[...] = state

    for h in range(GLA_HEADS):
        o = o_heads[h] + jnp.concatenate(inter[h], axis=0)
        o = _rms_rows(o) * og_ref[...]
        r = r_ref[h].astype(F32)
        o_ref[h] = (o * (r * jax.nn.sigmoid(r))).astype(o_ref.dtype)


def gla(proj, a_up, a_b, o_gain, *, batch, seq):
    rows = GLA_ROWS
    assert seq % rows == 0
    nb = seq // rows
    a_up_p = jnp.zeros((LANES, GLA_KW), BF16).at[:GLA_GATE_RANK].set(a_up.astype(BF16))
    row_blk = lambda n, s0: pl.BlockSpec((n, rows, LANES), lambda b, i: (s0 // n, b * nb + i, 0))
    return pl.pallas_call(
        _gla_kernel,
        grid=(batch, nb),
        in_specs=[
            row_blk(2, SLAB_GLA_Q), row_blk(2, SLAB_GLA_K), row_blk(4, SLAB_GLA_V),
            row_blk(1, SLAB_GLA_A), row_blk(4, SLAB_GLA_R),
            pl.BlockSpec((LANES, GLA_KW), lambda b, i: (0, 0)),
            pl.BlockSpec((1, GLA_KW), lambda b, i: (0, 0)),
            pl.BlockSpec((1, GLA_DV), lambda b, i: (0, 0)),
        ],
        out_specs=pl.BlockSpec((GLA_HEADS, rows, LANES), lambda b, i: (0, b * nb + i, 0)),
        out_shape=jax.ShapeDtypeStruct((GLA_HEADS, batch * seq, LANES), BF16),
        scratch_shapes=[pltpu.VMEM((GLA_KW, GLA_DV), F32)],
        compiler_params=_params("parallel", "arbitrary"),
        name="gla",
    )(proj, proj, proj, proj, proj, a_up_p,
      a_b.reshape(1, GLA_KW).astype(F32), o_gain.reshape(1, GLA_DV).astype(F32))


def _merge_kernel(x_ref, oa_ref, ob_ref, cq_ref, gate_ref, mkv_ref, cqg_ref, ckg_ref,
                  wsb_ref, wgla_ref, wca_ref, wout_ref, o_ref):
    ca_w = CA_HEADS * CA_HEAD_DIM
    oc = []
    for h in range(CA_HEADS):
        q = _rms_rows(cq_ref[h].astype(F32)) * (cqg_ref[...] * CA_HEAD_DIM ** -0.5)
        kk = _rms_rows(mkv_ref[:, h * CA_HEAD_DIM:(h + 1) * CA_HEAD_DIM].astype(F32)) * ckg_ref[...]
        s = lax.dot_general(q.astype(BF16), kk.astype(BF16), (((1,), (1,)), ((), ())),
                            preferred_element_type=F32)
        e = jnp.exp(s - jnp.max(s, axis=-1, keepdims=True))
        p = e * (1.0 / jnp.sum(e, axis=-1, keepdims=True))
        vv = mkv_ref[:, ca_w + h * CA_HEAD_DIM: ca_w + (h + 1) * CA_HEAD_DIM]
        oc.append(jnp.dot(p.astype(BF16), vv, preferred_element_type=F32).astype(BF16))
    o_c = jnp.concatenate(oc, axis=-1)
    o_a = jnp.concatenate([oa_ref[s] for s in range(4)], axis=-1)
    o_b = jnp.concatenate([ob_ref[s] for s in range(4)], axis=-1)

    def gate(b):
        g = jnp.concatenate([gate_ref[b * 8 + s] for s in range(8)], axis=-1).astype(F32)
        return jax.nn.sigmoid(g)

    merged = gate(0) * jnp.dot(o_a, wsb_ref[...], preferred_element_type=F32)
    merged += gate(1) * jnp.dot(o_b, wgla_ref[...], preferred_element_type=F32)
    merged += gate(2) * jnp.dot(o_c, wca_ref[...], preferred_element_type=F32)
    o_ref[...] = x_ref[...] + jnp.dot(merged.astype(BF16), wout_ref[...], preferred_element_type=F32)


def merge_branches(x, o_a, o_b, proj, mkv, cq_gain, ck_gain, w_sb, w_gla, w_ca, w_out, *, batch, seq, tm):
    n, d = x.shape
    tm = min(tm, seq)
    assert seq % tm == 0
    nt = seq // tm
    mem_len = mkv.shape[0] // batch
    full = lambda a: pl.BlockSpec(a.shape, lambda i: (0,) * a.ndim)
    cqg = cq_gain.reshape(1, CA_HEAD_DIM).astype(F32)
    ckg = ck_gain.reshape(1, CA_HEAD_DIM).astype(F32)
    return pl.pallas_call(
        _merge_kernel,
        grid=(n // tm,),
        in_specs=[
            pl.BlockSpec((tm, d), lambda i: (i, 0)),
            pl.BlockSpec((4, tm, LANES), lambda i: (0, i, 0)),
            pl.BlockSpec((4, tm, LANES), lambda i: (0, i, 0)),
            pl.BlockSpec((4, tm, LANES), lambda i: (SLAB_CA_Q // 4, i, 0)),
            pl.BlockSpec((24, tm, LANES), lambda i: (SLAB_GATES // 24, i, 0)),
            pl.BlockSpec((mem_len, mkv.shape[1]), lambda i: (i // nt, 0)),
            full(cqg), full(ckg), full(w_sb), full(w_gla), full(w_ca), full(w_out),
        ],
        out_specs=pl.BlockSpec((tm, d), lambda i: (i, 0)),
        out_shape=jax.ShapeDtypeStruct((n, d), F32),
        compiler_params=_params("parallel"),
        name="merge_branches",
    )(x, o_a, o_b, proj, proj, mkv,
      cqg, ckg, w_sb, w_gla, w_ca, w_out)


def _ffn_kernel(x_ref, g_ref, wg_ref, wu_ref, wd_ref, o_ref, hn_ref, acc_ref):
    f = pl.program_id(1)

    @pl.when(f == 0)
    def _():
        hn_ref[...] = (_rms_rows(x_ref[...]) * g_ref[...]).astype(BF16)
        acc_ref[...] = jnp.zeros_like(acc_ref)

    h = hn_ref[...]
    gt = jnp.dot(h, wg_ref[...], preferred_element_type=F32)
    up = jnp.dot(h, wu_ref[...], preferred_element_type=F32)
    act = (gt * jax.nn.sigmoid(gt) * up).astype(BF16)
    acc_ref[...] += jnp.dot(act, wd_ref[...], preferred_element_type=F32)

    @pl.when(f == pl.num_programs(1) - 1)
    def _():
        o_ref[...] = x_ref[...] + acc_ref[...]


def ffn_dense(x, g, w_gate_up, w_down, *, tm, tf):
    n, d = x.shape
    d_ff = w_down.shape[0]
    tm = min(tm, n)
    assert n % tm == 0 and d_ff % tf == 0 and tf % LANES == 0
    nf = d_ff // tf
    return pl.pallas_call(
        _ffn_kernel,
        grid=(n // tm, nf),
        in_specs=[
            pl.BlockSpec((tm, d), lambda i, f: (i, 0)),
            pl.BlockSpec((1, d), lambda i, f: (0, 0)),
            pl.BlockSpec((d, tf), lambda i, f: (0, f)),
            pl.BlockSpec((d, tf), lambda i, f: (0, nf + f)),
            pl.BlockSpec((tf, d), lambda i, f: (f, 0)),
        ],
        out_specs=pl.BlockSpec((tm, d), lambda i, f: (i, 0)),
        out_shape=jax.ShapeDtypeStruct((n, d), F32),
        scratch_shapes=[pltpu.VMEM((tm, d), BF16), pltpu.VMEM((tm, d), F32)],
        compiler_params=_params("parallel", "arbitrary"),
        name="ffn_dense",
    )(x, g.reshape(1, d), w_gate_up, w_gate_up, w_down)


def _router_kernel(x_ref, g_ref, wr_ref, c_ref):
    hn = _rms_rows(x_ref[...]) * g_ref[...]
    logits = jnp.dot(hn, wr_ref[...], preferred_element_type=F32, precision=lax.Precision.HIGHEST)
    lane = lax.broadcasted_iota(jnp.int32, logits.shape, 1).astype(F32)
    neg = jnp.float32(-jnp.inf)
    logits = jnp.where(lane < N_EXPERTS, logits, neg)
    m1 = jnp.max(logits, axis=-1, keepdims=True)
    i1 = jnp.min(jnp.where(logits == m1, lane, float(LANES)), axis=-1, keepdims=True)
    rest = jnp.where(lane == i1, neg, logits)
    m2 = jnp.max(rest, axis=-1, keepdims=True)
    i2 = jnp.min(jnp.where(rest == m2, lane, float(LANES)), axis=-1, keepdims=True)
    e2 = jnp.exp(m2 - m1)
    inv = 1.0 / (1.0 + e2)
    c_ref[...] = (jnp.where(lane == 0.0, inv, 0.0) + jnp.where(lane == 1.0, e2 * inv, 0.0)
                  + jnp.where(lane == 2.0, i1, 0.0) + jnp.where(lane == 3.0, i2, 0.0))


def router(x, g, w_router, *, tm):
    n, d = x.shape
    tm = min(tm, n)
    wr = jnp.zeros((d, LANES), F32).at[:, :N_EXPERTS].set(w_router)
    return pl.pallas_call(
        _router_kernel,
        grid=(n // tm,),
        in_specs=[
            pl.BlockSpec((tm, d), lambda i: (i, 0)),
            pl.BlockSpec((1, d), lambda i: (0, 0)),
            pl.BlockSpec((d, LANES), lambda i: (0, 0)),
        ],
        out_specs=pl.BlockSpec((tm, LANES), lambda i: (i, 0)),
        out_shape=jax.ShapeDtypeStruct((n, LANES), F32),
        compiler_params=_params("parallel"),
        name="router",
    )(x, g.reshape(1, d), wr)


def _row_copy(src_hbm, row, dst_vmem, r, sem):
    return pltpu.make_async_copy(src_hbm.at[pl.ds(row, 1)], dst_vmem.at[pl.ds(r, 1)], sem)


def _expert_ffn_kernel(te_ref, nu_ref, cur_ref, nxt_ref, x_hbm, g_ref, wg_ref, wu_ref, wd_ref, o_ref,
                       xbuf_ref, sem, hn_ref, acc_ref, *, nf):
    i = pl.program_id(0)
    f = pl.program_id(1)
    n_tiles = pl.num_programs(0)
    rows = o_ref.shape[0]
    share = rows // nf
    n_used = nu_ref[0]
    used = i < n_used
    slot = i % 2

    def drain(s):
        def one(r, carry):
            _row_copy(x_hbm, 0, xbuf_ref.at[s], r, sem.at[s]).wait()
            return carry
        lax.fori_loop(0, rows, one, 0, unroll=GATHER_UNROLL)

    @pl.when((i == 0) & (f == 0))
    def _():
        def one(r, carry):
            _row_copy(x_hbm, cur_ref[0, 0, r], xbuf_ref.at[0], r, sem.at[0]).start()
            return carry
        lax.fori_loop(0, rows, one, 0, unroll=GATHER_UNROLL)

    @pl.when((i <= n_used) & (f == 0))
    def _():
        drain(slot)

    @pl.when(used & (f == 0))
    def _():
        hn_ref[...] = (_rms_rows(xbuf_ref[slot]) * g_ref[...]).astype(BF16)
        acc_ref[...] = jnp.zeros_like(acc_ref)

    @pl.when(used)
    def _():
        for k in range(share):
            r = f * share + k
            _row_copy(x_hbm, nxt_ref[0, 0, r], xbuf_ref.at[1 - slot], r, sem.at[1 - slot]).start()
        h = hn_ref[...]
        gt = jnp.dot(h, wg_ref[0], preferred_element_type=F32)
        up = jnp.dot(h, wu_ref[0], preferred_element_type=F32)
        act = (gt * jax.nn.sigmoid(gt) * up).astype(BF16)
        acc_ref[...] += jnp.dot(act, wd_ref[0], preferred_element_type=F32)

    @pl.when(f == nf - 1)
    def _():
        o_ref[...] = jnp.where(used, acc_ref[...], 0.0)

    @pl.when(used & (i == n_tiles - 1) & (f == nf - 1))
    def _():
        drain(1 - slot)


def expert_ffn(x, src, g, tile_expert, n_used, w_gate_up, w_down, *, tm, tf):
    rows = src.shape[0]
    d = x.shape[1]
    d_ff = w_down.shape[1]
    assert rows % tm == 0 and d_ff % tf == 0 and tf % LANES == 0
    nf = d_ff // tf
    n_tiles = rows // tm
    src = src.reshape(n_tiles, 1, tm)
    return pl.pallas_call(
        functools.partial(_expert_ffn_kernel, nf=nf),
        grid_spec=pltpu.PrefetchScalarGridSpec(
            num_scalar_prefetch=2,
            grid=(n_tiles, nf),
            in_specs=[
                pl.BlockSpec((1, 1, tm), lambda i, f, te, nu: (i, 0, 0), memory_space=pltpu.SMEM),
                pl.BlockSpec((1, 1, tm), lambda i, f, te, nu: (jnp.minimum(i + 1, n_tiles - 1), 0, 0),
                             memory_space=pltpu.SMEM),
                pl.BlockSpec(memory_space=pl.ANY),
                pl.BlockSpec((1, d), lambda i, f, te, nu: (0, 0)),
                pl.BlockSpec((1, d, tf), lambda i, f, te, nu: (te[i], 0, f)),
                pl.BlockSpec((1, d, tf), lambda i, f, te, nu: (te[i], 0, nf + f)),
                pl.BlockSpec((1, tf, d), lambda i, f, te, nu: (te[i], f, 0)),
            ],
            out_specs=pl.BlockSpec((tm, d), lambda i, f, te, nu: (i, 0)),
            scratch_shapes=[pltpu.VMEM((2, tm, d), F32), pltpu.SemaphoreType.DMA((2,)),
                            pltpu.VMEM((tm, d), BF16), pltpu.VMEM((tm, d), F32)],
        ),
        out_shape=jax.ShapeDtypeStruct((rows, d), F32),
        compiler_params=_params("arbitrary", "arbitrary"),
        name="expert_ffn",
    )(tile_expert, n_used, src, src, x, g.reshape(1, d), w_gate_up, w_gate_up, w_down)


def _combine_kernel(p1_ref, p2_ref, n1_ref, n2_ref, x_ref, rw_ref, ys_hbm, o_ref, a_ref, b_ref, sem):
    i = pl.program_id(0)
    rows = o_ref.shape[0]
    slot = i % 2

    def issue(q1_ref, q2_ref, s):
        def one(r, carry):
            _row_copy(ys_hbm, q1_ref[0, 0, r], a_ref.at[s], r, sem.at[0, s]).start()
            _row_copy(ys_hbm, q2_ref[0, 0, r], b_ref.at[s], r, sem.at[1, s]).start()
            return carry
        lax.fori_loop(0, rows, one, 0, unroll=GATHER_UNROLL)

    @pl.when(i == 0)
    def _():
        issue(p1_ref, p2_ref, 0)

    @pl.when(i + 1 < pl.num_programs(0))
    def _():
        issue(n1_ref, n2_ref, 1 - slot)

    def drain(r, carry):
        _row_copy(ys_hbm, 0, a_ref.at[slot], r, sem.at[0, slot]).wait()
        _row_copy(ys_hbm, 0, b_ref.at[slot], r, sem.at[1, slot]).wait()
        return carry
    lax.fori_loop(0, rows, drain, 0, unroll=GATHER_UNROLL)

    rw = rw_ref[...]
    o_ref[...] = x_ref[...] + (rw[:, 0:1] * a_ref[slot] + rw[:, 1:2] * b_ref[slot])


def moe_combine(x, route, ys, pos1, pos2, *, tc):
    n, d = x.shape
    assert n % tc == 0
    nt = n // tc
    idx_spec = pl.BlockSpec((1, 1, tc), lambda i: (i, 0, 0), memory_space=pltpu.SMEM)
    nxt_spec = pl.BlockSpec((1, 1, tc), lambda i: (jnp.minimum(i + 1, nt - 1), 0, 0), memory_space=pltpu.SMEM)
    pos1 = pos1.reshape(nt, 1, tc)
    pos2 = pos2.reshape(nt, 1, tc)
    return pl.pallas_call(
        _combine_kernel,
        grid=(nt,),
        in_specs=[
            idx_spec, idx_spec, nxt_spec, nxt_spec,
            pl.BlockSpec((tc, d), lambda i: (i, 0)),
            pl.BlockSpec((tc, LANES), lambda i: (i, 0)),
            pl.BlockSpec(memory_space=pl.ANY),
        ],
        out_specs=pl.BlockSpec((tc, d), lambda i: (i, 0)),
        out_shape=jax.ShapeDtypeStruct((n, d), F32),
        scratch_shapes=[pltpu.VMEM((2, tc, d), F32), pltpu.VMEM((2, tc, d), F32),
                        pltpu.SemaphoreType.DMA((2, 2))],
        compiler_params=_params("arbitrary"),
        name="moe_combine",
    )(pos1, pos2, pos1, pos2, x, route, ys)


def _dispatch_plan(route, *, tm):
    n = route.shape[0]
    ids = route[:, 2:4].astype(jnp.int32)
    e_flat = ids.T.reshape(-1)
    onehot = (e_flat[:, None] == jnp.arange(N_EXPERTS)[None, :]).astype(jnp.int32)
    running = jnp.cumsum(onehot, axis=0)
    rank = jnp.sum(running * onehot, axis=1) - 1
    counts = running[-1]
    padded = ((counts + tm - 1) // tm) * tm
    ends = jnp.cumsum(padded)
    starts = ends - padded
    pos = starts[e_flat] + rank
    rows = TOP_K * n + N_EXPERTS * tm
    tok = jnp.tile(jnp.arange(n, dtype=jnp.int32), TOP_K)
    src = jnp.zeros((rows,), jnp.int32).at[pos].set(tok)
    tile_start = jnp.arange(rows // tm, dtype=jnp.int32) * tm
    tile_expert = jnp.minimum(jnp.sum(tile_start[:, None] >= ends[None, :], axis=1), N_EXPERTS - 1)
    n_used = (ends[-1] // tm).reshape(1).astype(jnp.int32)
    return src, tile_expert.astype(jnp.int32), n_used, pos[:n].astype(jnp.int32), pos[n:].astype(jnp.int32)


def ffn_moe(x, g, route, w_gate_up, w_down, *, tm, tf):
    src, tile_expert, n_used, pos1, pos2 = _dispatch_plan(route, tm=tm)
    ys = expert_ffn(x, src, g, tile_expert, n_used, w_gate_up, w_down, tm=tm, tf=tf)
    return moe_combine(x, route, ys, pos1, pos2, tc=min(256, x.shape[0]))


def _slab_weights(w_in):
    d = w_in.shape[0]
    sbw = SB_HEADS * SB_HEAD_DIM
    glv = GLA_HEADS * GLA_DV
    caw = CA_HEADS * CA_HEAD_DIM
    sizes = (sbw, sbw, sbw, GLA_KW, GLA_KW, glv, GLA_GATE_RANK, glv, caw, N_BRANCH * D_MODEL)
    parts, off = [], 0
    for s in sizes:
        parts.append(w_in[:, off:off + s])
        off += s
    sb_q, sb_k, sb_v, g_q, g_k, g_v, g_a, g_r, ca_q, gates = parts
    pad = lambda cols: jnp.zeros((d, cols), w_in.dtype)
    cols = [sb_q, sb_k, sb_v, g_v, g_r, ca_q, gates, g_q, g_k, g_a, pad(2 * LANES - GLA_GATE_RANK)]
    w = jnp.concatenate(cols, axis=1).astype(BF16)
    assert w.shape[1] == N_SLABS * LANES
    return w


def kernel(x, mem, mix_norm_g, w_in, sb_q_g, sb_k_g, gla_a_up, gla_a_b, gla_o_g, ca_q_g, ca_k_g, mem_norm_g, w_mem_kv, w_br_sb, w_br_gla, w_br_ca, w_out, ffn_norm_g, ff_w_gate_up, ff_w_down, moe_w_router, moe_w_gate_up, moe_w_down):
    batch, seq, d = x.shape
    depth = w_in.shape[0]
    n = batch * seq
    xf = x.reshape(n, d)
    memf = mem.reshape(-1, d)
    bf = lambda a: a.astype(BF16)

    for layer in range(depth):
        proj = norm_matmul(xf, mix_norm_g[layer], _slab_weights(w_in[layer]), tm=512, tn=2304, slabs=True)
        mkv = norm_matmul(memf, mem_norm_g[layer], bf(w_mem_kv[layer]), tm=512, tn=1024, slabs=False)
        o_a = sb_attention(proj, sb_q_g[layer], sb_k_g[layer], batch=batch, seq=seq, tq=256, tk=128)
        o_b = gla(proj, gla_a_up[layer], gla_a_b[layer], gla_o_g[layer], batch=batch, seq=seq)
        xf = merge_branches(xf, o_a, o_b, proj, mkv, ca_q_g[layer], ca_k_g[layer],
                            bf(w_br_sb[layer]), bf(w_br_gla[layer]), bf(w_br_ca[layer]), bf(w_out[layer]),
                            batch=batch, seq=seq, tm=512)
        j = layer // 2
        if layer % 2 == 0:
            xf = ffn_dense(xf, ffn_norm_g[layer], bf(ff_w_gate_up[j]), bf(ff_w_down[j]), tm=512, tf=1408)
        else:
            route = router(xf, ffn_norm_g[layer], moe_w_router[j], tm=512)
            xf = ffn_moe(xf, ffn_norm_g[layer], route, bf(moe_w_gate_up[j]), bf(moe_w_down[j]), tm=512, tf=1792)
    return xf.reshape(batch, seq, d)
```

```python
import functools

import jax
import jax.numpy as jnp
from jax import lax
from jax.experimental import pallas as pl
from jax.experimental.pallas import tpu as pltpu

F32 = jnp.float32
BF16 = jnp.bfloat16

LANES = 128
EPS = 1e-6
LOG2E = 1.4426950408889634
LN2 = 0.6931471805599453
MASKED = 1e30
GATHER_UNROLL = 8
SB_STAGES = 4
SB_EXIT = -110.0
VMEM_LIMIT_BYTES = 56 * 1024 * 1024

D_MODEL = 1024
SB_HEADS, SB_HEAD_DIM = 8, 64
GLA_HEADS, GLA_DK, GLA_DV = 4, 64, 128
GLA_GATE_RANK, GLA_GATE_TAU, GLA_CHUNK = 16, 16.0, 64
CA_HEADS, CA_HEAD_DIM = 4, 128
N_BRANCH = 3
N_EXPERTS, TOP_K = 8, 2

SLAB_SB_Q, SLAB_SB_K, SLAB_SB_V = 0, 4, 8
SLAB_GLA_V, SLAB_GLA_R = 12, 16
SLAB_CA_Q = 20
SLAB_GATES = 24
SLAB_GLA_Q, SLAB_GLA_K, SLAB_GLA_A = 48, 50, 52
N_SLABS = 54


def _params(*semantics):
    return pltpu.CompilerParams(dimension_semantics=semantics, vmem_limit_bytes=VMEM_LIMIT_BYTES)


def _rms_rows(x, eps=EPS):
    return x * lax.rsqrt(jnp.mean(x * x, axis=-1, keepdims=True) + eps)


def _rms_half_lanes(x, eps=EPS):
    lane = lax.broadcasted_iota(jnp.int32, x.shape, 1)
    lo = lane < SB_HEAD_DIM
    sq = x * x
    s_lo = jnp.sum(jnp.where(lo, sq, 0.0), axis=-1, keepdims=True)
    s_hi = jnp.sum(jnp.where(lo, 0.0, sq), axis=-1, keepdims=True)
    ms = jnp.where(lo, s_lo, s_hi) * (1.0 / SB_HEAD_DIM)
    return x * lax.rsqrt(ms + eps)


def _split_bf16(x):
    hi = x.astype(BF16)
    lo = (x - hi.astype(F32)).astype(BF16)
    return hi, lo


def _norm_matmul_kernel(x_ref, g_ref, w_ref, o_ref, xn_ref, *, n_slab):
    @pl.when(pl.program_id(1) == 0)
    def _():
        xn_ref[...] = (_rms_rows(x_ref[...]) * g_ref[...]).astype(BF16)

    acc = jnp.dot(xn_ref[...], w_ref[...], preferred_element_type=F32)
    if n_slab is None:
        o_ref[...] = acc.astype(o_ref.dtype)
    else:
        for s in range(n_slab):
            o_ref[s] = acc[:, s * LANES:(s + 1) * LANES].astype(o_ref.dtype)


def norm_matmul(x, g, w, *, tm, tn, slabs):
    n, k = x.shape
    m = w.shape[1]
    tm = min(tm, n)
    tn = min(tn, m)
    assert n % tm == 0 and m % tn == 0 and tn % LANES == 0
    if slabs:
        n_slab = tn // LANES
        out_shape = jax.ShapeDtypeStruct((m // LANES, n, LANES), BF16)
        out_spec = pl.BlockSpec((n_slab, tm, LANES), lambda i, j: (j, i, 0))
    else:
        n_slab = None
        out_shape = jax.ShapeDtypeStruct((n, m), BF16)
        out_spec = pl.BlockSpec((tm, tn), lambda i, j: (i, j))
    return pl.pallas_call(
        functools.partial(_norm_matmul_kernel, n_slab=n_slab),
        grid=(n // tm, m // tn),
        in_specs=[
            pl.BlockSpec((tm, k), lambda i, j: (i, 0)),
            pl.BlockSpec((1, k), lambda i, j: (0, 0)),
            pl.BlockSpec((k, tn), lambda i, j: (0, j)),
        ],
        out_specs=out_spec,
        out_shape=out_shape,
        scratch_shapes=[pltpu.VMEM((tm, k), BF16)],
        compiler_params=_params("parallel", "arbitrary"),
        name="norm_matmul",
    )(x, g.reshape(1, k), w)


def _softplus(z):
    return jnp.maximum(z, 0.0) + jnp.log(1.0 + jnp.exp(-jnp.abs(z)))


def _sb_items(nq, n_diag):
    qi, kj = [], []
    for i in range(nq):
        for j in reversed(range((i + 1) * n_diag)):
            qi.append(i)
            kj.append(j)
    n_items = len(qi)
    tile_start = [i * (i + 1) // 2 * n_diag for i in range(nq + 1)]
    assert tile_start[nq] == n_items
    qi += [0] * n_diag * SB_STAGES
    kj += list(reversed(range(n_diag))) * SB_STAGES
    return n_items, qi, kj, tile_start


def _sb_kernel(qi_tab, kj_tab, ts_tab, q_ref, k_ref, v_ref, qg_ref, kg_ref, o_ref,
               qh_ref, kn_ref, z_ref, sp_ref, bt_ref, w_ref, rs_ref, *, tq, tk, seq, n_items):
    n_diag = tq // tk
    rows = min(512, seq)
    lane = lax.broadcasted_iota(jnp.int32, (rows, LANES), 1)
    scale = SB_HEAD_DIM ** -0.5

    def norm_chunk(c, carry):
        r0 = pl.multiple_of(c * rows, rows)
        kk = k_ref[0, pl.ds(r0, rows), :].astype(F32)
        kn_ref[pl.ds(r0, rows), :] = (_rms_half_lanes(kk) * kg_ref[...]).astype(BF16)
        qn = _rms_half_lanes(q_ref[0, pl.ds(r0, rows), :].astype(F32)) * (qg_ref[...] * scale)
        qh_ref[0, pl.ds(r0, rows), :] = jnp.where(lane < SB_HEAD_DIM, qn, 0.0).astype(BF16)
        qh_ref[1, pl.ds(r0, rows), :] = jnp.where(lane >= SB_HEAD_DIM, qn, 0.0).astype(BF16)
        return carry
    lax.fori_loop(0, seq // rows, norm_chunk, 0)

    z_ref[...] = jnp.zeros_like(z_ref)
    sp_ref[...] = jnp.zeros_like(sp_ref)
    bt_ref[...] = jnp.zeros_like(bt_ref)
    w_ref[...] = jnp.zeros_like(w_ref)
    rs_ref[...] = jnp.zeros_like(rs_ref)
    o_ref[...] = jnp.zeros_like(o_ref)

    rj = lax.broadcasted_iota(jnp.int32, (tk, tk), 0)
    cs = lax.broadcasted_iota(jnp.int32, (tk, tk), 1)
    u_strict = jnp.where(rj > cs, -1.0, 0.0).astype(BF16)
    rel = (lax.broadcasted_iota(jnp.int32, (tq, tk), 1) - lax.broadcasted_iota(jnp.int32, (tq, tk), 0))
    out_lane = lax.broadcasted_iota(jnp.int32, (tq, LANES), 1) < SB_HEAD_DIM

    def item(m):
        return qi_tab[m], kj_tab[m]

    def step(slot, ids, carries, accs):
        prev = 1 - slot
        qi1, kj1 = item(ids[0])
        kb = kn_ref[pl.ds(pl.multiple_of(kj1 * tk, tk), tk), :]
        for h in range(2):
            qh = qh_ref[h, pl.ds(pl.multiple_of(qi1 * tq, tq), tq), :]
            z_ref[slot, h] = lax.dot_general(qh, kb, (((1,), (1,)), ((), ())), preferred_element_type=F32)
        qi2, kj2 = item(ids[1])
        keep = rel < (qi2 * tq - kj2 * tk)
        for h in range(2):
            z = jnp.where(keep, z_ref[prev, h], -MASKED)
            e = jnp.exp2(jnp.abs(z) * (-LOG2E))
            r = pl.reciprocal(1.0 + e)
            sp =jnp.maximum(z, 0.0) - jnp.log(r)
            beta = jnp.where(z >= 0.0, r, e * r)
            sp_ref[slot, h] = sp.astype(BF16)
            bt_ref[slot, h] = beta.astype(BF16)
            rs_ref[slot, h] = jnp.broadcast_to(jnp.sum(sp, axis=1, keepdims=True), (tq, LANES))
        qi3, kj3 = item(ids[2])
        first3 = kj3 == (qi3 + 1) * n_diag - 1
        new_carries = []
        for h in range(2):
            t = jnp.dot(sp_ref[prev, h], u_strict, preferred_element_type=F32)
            c_in = jnp.where(first3, 0.0, carries[h])
            c_wide = jnp.concatenate([c_in] * (tk // LANES), axis=1)
            w = jnp.exp2((t + c_wide) * LOG2E) * bt_ref[prev, h].astype(F32)
            w_ref[slot, h] = w.astype(BF16)
            new_carries.append(c_in - rs_ref[prev, h])
        qi4, kj4 = item(ids[3])
        first4 = kj4 == (qi4 + 1) * n_diag - 1
        vb = v_ref[0, pl.ds(pl.multiple_of(kj4 * tk, tk), tk), :]
        new_accs = []
        for h in range(2):
            pv = jnp.dot(w_ref[prev, h], vb, preferred_element_type=F32)
            new_accs.append(jnp.where(first4, 0.0, accs[h]) + pv)
        out_rows = pl.ds(pl.multiple_of(qi4 * tq, tq), tq)
        result = jnp.where(out_lane, new_accs[0], new_accs[1]).astype(o_ref.dtype)
        o_ref[0, out_rows, :] = jnp.where(ids[3] < n_items, result, o_ref[0, out_rows, :])
        return new_carries, new_accs

    def advance(slot, st):
        ptr, in2, in3, in4 = st[:4]
        carries, accs = [st[4], st[5]], [st[6], st[7]]
        in1 = jnp.minimum(ptr, n_items)
        carries, accs = step(slot, (in1, in2, in3, in4), carries, accs)
        nxt = jnp.minimum(ptr + 1, n_items)
        tile3 = qi_tab[in3]
        done = jnp.max(jnp.maximum(carries[0], carries[1])) < SB_EXIT
        jump = done & (in3 < n_items) & (nxt < n_items) & (qi_tab[nxt] == tile3)
        ptr = jnp.where(jump, ts_tab[tile3 + 1], nxt)
        return (ptr, in1, in2, in3, carries[0], carries[1], accs[0], accs[1])

    def body(st):
        return advance(1, advance(0, st))

    def pending(st):
        ptr, in2, in3, in4 = st[:4]
        return (ptr < n_items) | (in2 < n_items) | (in3 < n_items) | (in4 < n_items)

    zeros = jnp.zeros((tq, LANES), F32)
    filler = jnp.int32(n_items)
    lax.while_loop(pending, body, (jnp.int32(0), filler, filler, filler, zeros, zeros, zeros, zeros))


def sb_attention(proj, q_gain, k_gain, *, batch, seq, tq, tk):
    tq = min(tq, seq)
    tk = min(tk, tq)
    assert seq % tq == 0 and tq % tk == 0
    n_pairs = SB_HEADS // 2
    n_items, qi, kj, tile_start = _sb_items(seq // tq, tq // tk)
    g2 = lambda g: jnp.concatenate([g, g]).reshape(1, LANES).astype(F32)
    seq_blk = lambda s0: pl.BlockSpec((1, seq, LANES), lambda b, p, qt, kt, ts: (s0 + p, b, 0))
    stage = lambda dt: pltpu.VMEM((2, 2, tq, tk), dt)
    return pl.pallas_call(
        functools.partial(_sb_kernel, tq=tq, tk=tk, seq=seq, n_items=n_items),
        grid_spec=pltpu.PrefetchScalarGridSpec(
            num_scalar_prefetch=3,
            grid=(batch, n_pairs),
            in_specs=[
                seq_blk(SLAB_SB_Q), seq_blk(SLAB_SB_K), seq_blk(SLAB_SB_V),
                pl.BlockSpec((1, LANES), lambda b, p, qt, kt, ts: (0, 0)),
                pl.BlockSpec((1, LANES), lambda b, p, qt, kt, ts: (0, 0)),
            ],
            out_specs=pl.BlockSpec((1, seq, LANES), lambda b, p, qt, kt, ts: (p, b, 0)),
            scratch_shapes=[
                pltpu.VMEM((2, seq, LANES), BF16),
                pltpu.VMEM((seq, LANES), BF16),
                stage(F32), stage(BF16), stage(BF16), stage(BF16),
                pltpu.VMEM((2, 2, tq, LANES), F32),
            ],
        ),
        out_shape=jax.ShapeDtypeStruct((n_pairs, batch * seq, LANES), BF16),
        compiler_params=_params("parallel", "parallel"),
        name="sb_attention",
    )(jnp.asarray(qi, jnp.int32), jnp.asarray(kj, jnp.int32), jnp.asarray(tile_start, jnp.int32), proj, proj, proj, g2(q_gain), g2(k_gain))


GLA_ROWS = 256
GLA_KW = GLA_HEADS * GLA_DK


def _gla_kernel(q_ref, k_ref, v_ref, a_ref, r_ref, aup_ref, ab_ref, og_ref, o_ref, state_ref):
    rows = GLA_ROWS
    n_chunks = rows // GLA_CHUNK

    @pl.when(pl.program_id(1) == 0)
    def _():
        state_ref[...] = jnp.zeros_like(state_ref)

    pre = jnp.dot(a_ref[0], aup_ref[...], preferred_element_type=F32) + ab_ref[...]
    log_a = -_softplus(-pre) * (1.0 / GLA_GATE_TAU)
    la_hi, la_lo = _split_bf16(log_a)

    ri = lax.broadcasted_iota(jnp.int32, (rows, rows), 0)
    ci = lax.broadcasted_iota(jnp.int32, (rows, rows), 1)
    same_chunk = (ri // GLA_CHUNK) == (ci // GLA_CHUNK)
    tri = same_chunk & (ri >= ci)
    tri_b = jnp.where(tri, 1.0, 0.0).astype(BF16)
    ones_b = jnp.where(same_chunk, 1.0, 0.0).astype(BF16)
    cum = (jnp.dot(tri_b, la_hi, preferred_element_type=F32)
           + jnp.dot(tri_b, la_lo, preferred_element_type=F32))
    cum_last = (jnp.dot(ones_b, la_hi, preferred_element_type=F32)
                + jnp.dot(ones_b, la_lo, preferred_element_type=F32))

    q_all = jnp.concatenate([q_ref[0], q_ref[1]], axis=-1).astype(F32)
    k_all = jnp.concatenate([k_ref[0], k_ref[1]], axis=-1).astype(F32)
    q_dec = q_all * (GLA_DK ** -0.5) * jnp.exp(cum)
    k_inv = (k_all * jnp.exp(-cum)).astype(BF16)
    k_end = (k_all * jnp.exp(cum_last - cum)).astype(BF16)
    v_all = jnp.concatenate([v_ref[h] for h in range(GLA_HEADS)], axis=-1)

    lane = lax.broadcasted_iota(jnp.int32, (rows, GLA_KW), 1)
    q_heads = [jnp.where((lane // GLA_DK) == h, q_dec, 0.0).astype(BF16) for h in range(GLA_HEADS)]

    o_heads = []
    for h in range(GLA_HEADS):
        sc = lax.dot_general(q_heads[h], k_inv, (((1,), (1,)), ((), ())), preferred_element_type=F32)
        sc = jnp.where(tri, sc, 0.0).astype(BF16)
        o_heads.append(jnp.dot(sc, v_ref[h], preferred_element_type=F32))

    ones_cols = jnp.ones((GLA_CHUNK, LANES), BF16)
    state = state_ref[...]
    inter = [[] for _ in range(GLA_HEADS)]
    for c in range(n_chunks):
        r0, r1 = c * GLA_CHUNK, (c + 1) * GLA_CHUNK
        state_b = state.astype(BF16)
        for h in range(GLA_HEADS):
            inter[h].append(jnp.dot(q_heads[h][r0:r1], state_b, preferred_element_type=F32))
        kv = lax.dot_general(k_end[r0:r1], v_all[r0:r1], (((0,), (0,)), ((), ())),
                             preferred_element_type=F32)
        kv_sel = jnp.concatenate(
            [kv[h * GLA_DK:(h + 1) * GLA_DK, h * GLA_DV:(h + 1) * GLA_DV] for h in range(GLA_HEADS)], axis=0)
        tot = (lax.dot_general(la_hi[r0:r1], ones_cols, (((0,), (0,)), ((), ())), preferred_element_type=F32)
               + lax.dot_general(la_lo[r0:r1], ones_cols, (((0,), (0,)), ((), ())), preferred_element_type=F32))
        state = jnp.exp(tot) * state + kv_sel
    state_ref[...] = state

    for h in range(GLA_HEADS):
        o = o_heads[h] + jnp.concatenate(inter[h], axis=0)
        o = _rms_rows(o) * og_ref[...]
        r = r_ref[h].astype(F32)
        o_ref[h] = (o * (r * jax.nn.sigmoid(r))).astype(o_ref.dtype)


def gla(proj, a_up, a_b, o_gain, *, batch, seq):
    rows = GLA_ROWS
    assert seq % rows == 0
    nb = seq // rows
    a_up_p = jnp.zeros((LANES, GLA_KW), BF16).at[:GLA_GATE_RANK].set(a_up.astype(BF16))
    row_blk = lambda n, s0: pl.BlockSpec((n, rows, LANES), lambda b, i: (s0 // n, b * nb + i, 0))
    return pl.pallas_call(
        _gla_kernel,
        grid=(batch, nb),
        in_specs=[
            row_blk(2, SLAB_GLA_Q), row_blk(2, SLAB_GLA_K), row_blk(4, SLAB_GLA_V),
            row_blk(1, SLAB_GLA_A), row_blk(4, SLAB_GLA_R),
            pl.BlockSpec((LANES, GLA_KW), lambda b, i: (0, 0)),
            pl.BlockSpec((1, GLA_KW), lambda b, i: (0, 0)),
            pl.BlockSpec((1, GLA_DV), lambda b, i: (0, 0)),
        ],
        out_specs=pl.BlockSpec((GLA_HEADS, rows, LANES), lambda b, i: (0, b * nb + i, 0)),
        out_shape=jax.ShapeDtypeStruct((GLA_HEADS, batch * seq, LANES), BF16),
        scratch_shapes=[pltpu.VMEM((GLA_KW, GLA_DV), F32)],
        compiler_params=_params("parallel", "arbitrary"),
        name="gla",
    )(proj, proj, proj, proj, proj, a_up_p,
      a_b.reshape(1, GLA_KW).astype(F32), o_gain.reshape(1, GLA_DV).astype(F32))


def _merge_kernel(x_ref, oa_ref, ob_ref, cq_ref, gate_ref, mkv_ref, cqg_ref, ckg_ref,
                  wsb_ref, wgla_ref, wca_ref, wout_ref, o_ref):
    ca_w = CA_HEADS * CA_HEAD_DIM
    oc = []
    for h in range(CA_HEADS):
        q = _rms_rows(cq_ref[h].astype(F32)) * (cqg_ref[...] * CA_HEAD_DIM ** -0.5)
        kk = _rms_rows(mkv_ref[:, h * CA_HEAD_DIM:(h + 1) * CA_HEAD_DIM].astype(F32)) * ckg_ref[...]
        s = lax.dot_general(q.astype(BF16), kk.astype(BF16), (((1,), (1,)), ((), ())),
                            preferred_element_type=F32)
        e = jnp.exp(s - jnp.max(s, axis=-1, keepdims=True))
        p = e * (1.0 / jnp.sum(e, axis=-1, keepdims=True))
        vv = mkv_ref[:, ca_w + h * CA_HEAD_DIM: ca_w + (h + 1) * CA_HEAD_DIM]
        oc.append(jnp.dot(p.astype(BF16), vv, preferred_element_type=F32).astype(BF16))
    o_c = jnp.concatenate(oc, axis=-1)
    o_a = jnp.concatenate([oa_ref[s] for s in range(4)], axis=-1)
    o_b = jnp.concatenate([ob_ref[s] for s in range(4)], axis=-1)

    def gate(b):
        g = jnp.concatenate([gate_ref[b * 8 + s] for s in range(8)], axis=-1).astype(F32)
        return jax.nn.sigmoid(g)

    merged = gate(0) * jnp.dot(o_a, wsb_ref[...], preferred_element_type=F32)
    merged += gate(1) * jnp.dot(o_b, wgla_ref[...], preferred_element_type=F32)
    merged += gate(2) * jnp.dot(o_c, wca_ref[...], preferred_element_type=F32)
    o_ref[...] = x_ref[...] + jnp.dot(merged.astype(BF16), wout_ref[...], preferred_element_type=F32)


def merge_branches(x, o_a, o_b, proj, mkv, cq_gain, ck_gain, w_sb, w_gla, w_ca, w_out, *, batch, seq, tm):
    n, d = x.shape
    tm = min(tm, seq)
    assert seq % tm == 0
    nt = seq // tm
    mem_len = mkv.shape[0] // batch
    full = lambda a: pl.BlockSpec(a.shape, lambda i: (0,) * a.ndim)
    cqg = cq_gain.reshape(1, CA_HEAD_DIM).astype(F32)
    ckg = ck_gain.reshape(1, CA_HEAD_DIM).astype(F32)
    return pl.pallas_call(
        _merge_kernel,
        grid=(n // tm,),
        in_specs=[
            pl.BlockSpec((tm, d), lambda i: (i, 0)),
            pl.BlockSpec((4, tm, LANES), lambda i: (0, i, 0)),
            pl.BlockSpec((4, tm, LANES), lambda i: (0, i, 0)),
            pl.BlockSpec((4, tm, LANES), lambda i: (SLAB_CA_Q // 4, i, 0)),
            pl.BlockSpec((24, tm, LANES), lambda i: (SLAB_GATES // 24, i, 0)),
            pl.BlockSpec((mem_len, mkv.shape[1]), lambda i: (i // nt, 0)),
            full(cqg), full(ckg), full(w_sb), full(w_gla), full(w_ca), full(w_out),
        ],
        out_specs=pl.BlockSpec((tm, d), lambda i: (i, 0)),
        out_shape=jax.ShapeDtypeStruct((n, d), F32),
        compiler_params=_params("parallel"),
        name="merge_branches",
    )(x, o_a, o_b, proj, proj, mkv,
      cqg, ckg, w_sb, w_gla, w_ca, w_out)


def _ffn_kernel(x_ref, g_ref, wg_ref, wu_ref, wd_ref, o_ref, hn_ref, acc_ref):
    f = pl.program_id(1)

    @pl.when(f == 0)
    def _():
        hn_ref[...] = (_rms_rows(x_ref[...]) * g_ref[...]).astype(BF16)
        acc_ref[...] = jnp.zeros_like(acc_ref)

    h = hn_ref[...]
    gt = jnp.dot(h, wg_ref[...], preferred_element_type=F32)
    up = jnp.dot(h, wu_ref[...], preferred_element_type=F32)
    act = (gt * jax.nn.sigmoid(gt) * up).astype(BF16)
    acc_ref[...] += jnp.dot(act, wd_ref[...], preferred_element_type=F32)

    @pl.when(f == pl.num_programs(1) - 1)
    def _():
        o_ref[...] = x_ref[...] + acc_ref[...]


def ffn_dense(x, g, w_gate_up, w_down, *, tm, tf):
    n, d = x.shape
    d_ff = w_down.shape[0]
    tm = min(tm, n)
    assert n % tm == 0 and d_ff % tf == 0 and tf % LANES == 0
    nf = d_ff // tf
    return pl.pallas_call(
        _ffn_kernel,
        grid=(n // tm, nf),
        in_specs=[
            pl.BlockSpec((tm, d), lambda i, f: (i, 0)),
            pl.BlockSpec((1, d), lambda i, f: (0, 0)),
            pl.BlockSpec((d, tf), lambda i, f: (0, f)),
            pl.BlockSpec((d, tf), lambda i, f: (0, nf + f)),
            pl.BlockSpec((tf, d), lambda i, f: (f, 0)),
        ],
        out_specs=pl.BlockSpec((tm, d), lambda i, f: (i, 0)),
        out_shape=jax.ShapeDtypeStruct((n, d), F32),
        scratch_shapes=[pltpu.VMEM((tm, d), BF16), pltpu.VMEM((tm, d), F32)],
        compiler_params=_params("parallel", "arbitrary"),
        name="ffn_dense",
    )(x, g.reshape(1, d), w_gate_up, w_gate_up, w_down)


def _router_kernel(x_ref, g_ref, wr_ref, c_ref):
    hn = _rms_rows(x_ref[...]) * g_ref[...]
    logits = jnp.dot(hn, wr_ref[...], preferred_element_type=F32, precision=lax.Precision.HIGHEST)
    lane = lax.broadcasted_iota(jnp.int32, logits.shape, 1).astype(F32)
    neg = jnp.float32(-jnp.inf)
    logits = jnp.where(lane < N_EXPERTS, logits, neg)
    m1 = jnp.max(logits, axis=-1, keepdims=True)
    i1 = jnp.min(jnp.where(logits == m1, lane, float(LANES)), axis=-1, keepdims=True)
    rest = jnp.where(lane == i1, neg, logits)
    m2 = jnp.max(rest, axis=-1, keepdims=True)
    i2 = jnp.min(jnp.where(rest == m2, lane, float(LANES)), axis=-1, keepdims=True)
    e2 = jnp.exp(m2 - m1)
    inv = 1.0 / (1.0 + e2)
    c_ref[...] = (jnp.where(lane == 0.0, inv, 0.0) + jnp.where(lane == 1.0, e2 * inv, 0.0)
                  + jnp.where(lane == 2.0, i1, 0.0) + jnp.where(lane == 3.0, i2, 0.0))


def router(x, g, w_router, *, tm):
    n, d = x.shape
    tm = min(tm, n)
    wr = jnp.zeros((d, LANES), F32).at[:, :N_EXPERTS].set(w_router)
    return pl.pallas_call(
        _router_kernel,
        grid=(n // tm,),
        in_specs=[
            pl.BlockSpec((tm, d), lambda i: (i, 0)),
            pl.BlockSpec((1, d), lambda i: (0, 0)),
            pl.BlockSpec((d, LANES), lambda i: (0, 0)),
        ],
        out_specs=pl.BlockSpec((tm, LANES), lambda i: (i, 0)),
        out_shape=jax.ShapeDtypeStruct((n, LANES), F32),
        compiler_params=_params("parallel"),
        name="router",
    )(x, g.reshape(1, d), wr)


def _row_copy(src_hbm, row, dst_vmem, r, sem):
    return pltpu.make_async_copy(src_hbm.at[pl.ds(row, 1)], dst_vmem.at[pl.ds(r, 1)], sem)


def _expert_ffn_kernel(te_ref, nu_ref, cur_ref, nxt_ref, x_hbm, g_ref, wg_ref, wu_ref, wd_ref, o_ref,
                       xbuf_ref, sem, hn_ref, acc_ref, *, nf):
    i = pl.program_id(0)
    f = pl.program_id(1)
    rows = o_ref.shape[0]
    used = i < nu_ref[0]
    slot = i % 2

    def normalise(s):
        hn_ref[s] = (_rms_rows(xbuf_ref[...]) * g_ref[...]).astype(BF16)

    @pl.when((i == 0) & (f == 0))
    def _():
        def start(r, carry):
            _row_copy(x_hbm, cur_ref[0, 0, r], xbuf_ref, r, sem).start()
            return carry
        lax.fori_loop(0, rows, start, 0, unroll=GATHER_UNROLL)

        def wait(r, carry):
            _row_copy(x_hbm, 0, xbuf_ref, r, sem).wait()
            return carry
        lax.fori_loop(0, rows, wait, 0, unroll=GATHER_UNROLL)
        normalise(0)

    def ffn_step(first, last):
        if first:
            for r in range(rows):
                _row_copy(x_hbm, nxt_ref[0, 0, r], xbuf_ref, r, sem).start()
        h = hn_ref[slot]
        gt = jnp.dot(h, wg_ref[0], preferred_element_type=F32)
        up = jnp.dot(h, wu_ref[0], preferred_element_type=F32)
        act = (gt * jax.nn.sigmoid(gt) * up).astype(BF16)
        part = jnp.dot(act, wd_ref[0], preferred_element_type=F32)
        total = part if first else acc_ref[...] + part
        if last:
            o_ref[...] = total
            for r in range(rows):
                _row_copy(x_hbm, 0, xbuf_ref, r, sem).wait()
            normalise(1 - slot)
        else:
            acc_ref[...] = total

    if nf == 1:
        pl.when(used)(lambda: ffn_step(True, True))
    else:
        pl.when(used & (f == 0))(lambda: ffn_step(True, False))
        pl.when(used & (f == nf - 1))(lambda: ffn_step(False, True))
        if nf > 2:
            pl.when(used & (f > 0) & (f < nf - 1))(lambda: ffn_step(False, False))

    @pl.when(jnp.logical_not(used) & (f == nf - 1))
    def _():
        o_ref[...] = jnp.zeros_like(o_ref)


def expert_ffn(x, src, g, tile_expert, n_used, w_gate_up, w_down, *, tm, tf):
    rows = src.shape[0]
    d = x.shape[1]
    d_ff = w_down.shape[1]
    assert rows % tm == 0 and d_ff % tf == 0 and tf % LANES == 0
    nf = d_ff // tf
    n_tiles = rows // tm
    src = src.reshape(n_tiles, 1, tm)
    return pl.pallas_call(
        functools.partial(_expert_ffn_kernel, nf=nf),
        grid_spec=pltpu.PrefetchScalarGridSpec(
            num_scalar_prefetch=2,
            grid=(n_tiles, nf),
            in_specs=[
                pl.BlockSpec((1, 1, tm), lambda i, f, te, nu: (i, 0, 0), memory_space=pltpu.SMEM),
                pl.BlockSpec((1, 1, tm), lambda i, f, te, nu: (jnp.minimum(i + 1, n_tiles - 1), 0, 0),
                             memory_space=pltpu.SMEM),
                pl.BlockSpec(memory_space=pl.ANY),
                pl.BlockSpec((1, d), lambda i, f, te, nu: (0, 0)),
                pl.BlockSpec((1, d, tf), lambda i, f, te, nu: (te[i], 0, f)),
                pl.BlockSpec((1, d, tf), lambda i, f, te, nu: (te[i], 0, nf + f)),
                pl.BlockSpec((1, tf, d), lambda i, f, te, nu: (te[i], f, 0)),
            ],
            out_specs=pl.BlockSpec((tm, d), lambda i, f, te, nu: (i, 0)),
            scratch_shapes=[pltpu.VMEM((tm, d), F32), pltpu.SemaphoreType.DMA,
                            pltpu.VMEM((2, tm, d), BF16), pltpu.VMEM((tm, d), F32)],
        ),
        out_shape=jax.ShapeDtypeStruct((rows, d), F32),
        compiler_params=_params("arbitrary", "arbitrary"),
        name="expert_ffn",
    )(tile_expert, n_used, src, src, x, g.reshape(1, d), w_gate_up, w_gate_up, w_down)


def _combine_kernel(p1_ref, p2_ref, n1_ref, n2_ref, x_ref, rw_ref, ys_hbm, o_ref, a_ref, b_ref, sem):
    i = pl.program_id(0)
    rows = o_ref.shape[0]
    slot = i % 2

    def issue(q1_ref, q2_ref, s):
        def one(r, carry):
            _row_copy(ys_hbm, q1_ref[0, 0, r], a_ref.at[s], r, sem.at[0, s]).start()
            _row_copy(ys_hbm, q2_ref[0, 0, r], b_ref.at[s], r, sem.at[1, s]).start()
            return carry
        lax.fori_loop(0, rows, one, 0, unroll=GATHER_UNROLL)

    @pl.when(i == 0)
    def _():
        issue(p1_ref, p2_ref, 0)

    @pl.when(i + 1 < pl.num_programs(0))
    def _():
        issue(n1_ref, n2_ref, 1 - slot)

    def drain(r, carry):
        _row_copy(ys_hbm, 0, a_ref.at[slot], r, sem.at[0, slot]).wait()
        _row_copy(ys_hbm, 0, b_ref.at[slot], r, sem.at[1, slot]).wait()
        return carry
    lax.fori_loop(0, rows, drain, 0, unroll=GATHER_UNROLL)

    rw = rw_ref[...]
    o_ref[...] = x_ref[...] + (rw[:, 0:1] * a_ref[slot] + rw[:, 1:2] * b_ref[slot])


def moe_combine(x, route, ys, pos1, pos2, *, tc):
    n, d = x.shape
    assert n % tc == 0
    nt = n // tc
    idx_spec = pl.BlockSpec((1, 1, tc), lambda i: (i, 0, 0), memory_space=pltpu.SMEM)
    nxt_spec = pl.BlockSpec((1, 1, tc), lambda i: (jnp.minimum(i + 1, nt - 1), 0, 0), memory_space=pltpu.SMEM)
    pos1 = pos1.reshape(nt, 1, tc)
    pos2 = pos2.reshape(nt, 1, tc)
    return pl.pallas_call(
        _combine_kernel,
        grid=(nt,),
        in_specs=[
            idx_spec, idx_spec, nxt_spec, nxt_spec,
            pl.BlockSpec((tc, d), lambda i: (i, 0)),
            pl.BlockSpec((tc, LANES), lambda i: (i, 0)),
            pl.BlockSpec(memory_space=pl.ANY),
        ],
        out_specs=pl.BlockSpec((tc, d), lambda i: (i, 0)),
        out_shape=jax.ShapeDtypeStruct((n, d), F32),
        scratch_shapes=[pltpu.VMEM((2, tc, d), F32), pltpu.VMEM((2, tc, d), F32),
                        pltpu.SemaphoreType.DMA((2, 2))],
        compiler_params=_params("arbitrary"),
        name="moe_combine",
    )(pos1, pos2, pos1, pos2, x, route, ys)


def _dispatch_plan(route, *, tm):
    n = route.shape[0]
    ids = route[:, 2:4].astype(jnp.int32)
    e_flat = ids.T.reshape(-1)
    onehot = (e_flat[:, None] == jnp.arange(N_EXPERTS)[None, :]).astype(jnp.int32)
    running = jnp.cumsum(onehot, axis=0)
    rank = jnp.sum(running * onehot, axis=1) - 1
    counts = running[-1]
    padded = ((counts + tm - 1) // tm) * tm
    ends = jnp.cumsum(padded)
    starts = ends - padded
    pos = starts[e_flat] + rank
    rows = TOP_K * n + N_EXPERTS * tm
    tok = jnp.tile(jnp.arange(n, dtype=jnp.int32), TOP_K)
    src = jnp.zeros((rows,), jnp.int32).at[pos].set(tok)
    tile_start = jnp.arange(rows // tm, dtype=jnp.int32) * tm
    tile_expert = jnp.minimum(jnp.sum(tile_start[:, None] >= ends[None, :], axis=1), N_EXPERTS - 1)
    n_used = (ends[-1] // tm).reshape(1).astype(jnp.int32)
    return src, tile_expert.astype(jnp.int32), n_used, pos[:n].astype(jnp.int32), pos[n:].astype(jnp.int32)


def ffn_moe(x, g, route, w_gate_up, w_down, *, tm, tf):
    src, tile_expert, n_used, pos1, pos2 = _dispatch_plan(route, tm=tm)
    ys = expert_ffn(x, src, g, tile_expert, n_used, w_gate_up, w_down, tm=tm, tf=tf)
    return moe_combine(x, route, ys, pos1, pos2, tc=min(256, x.shape[0]))


def _slab_weights(w_in):
    d = w_in.shape[0]
    sbw = SB_HEADS * SB_HEAD_DIM
    glv = GLA_HEADS * GLA_DV
    caw = CA_HEADS * CA_HEAD_DIM
    sizes = (sbw, sbw, sbw, GLA_KW, GLA_KW, glv, GLA_GATE_RANK, glv, caw, N_BRANCH * D_MODEL)
    parts, off = [], 0
    for s in sizes:
        parts.append(w_in[:, off:off + s])
        off += s
    sb_q, sb_k, sb_v, g_q, g_k, g_v, g_a, g_r, ca_q, gates = parts
    pad = jnp.zeros((d, 2 * LANES - GLA_GATE_RANK), w_in.dtype)
    cols = [sb_q, sb_k, sb_v, g_v, g_r, ca_q, gates, g_q, g_k, g_a, pad]
    w = jnp.concatenate([c.astype(BF16) for c in cols], axis=1)
    assert w.shape[1] == N_SLABS * LANES
    return w


def kernel(x, mem, mix_norm_g, w_in, sb_q_g, sb_k_g, gla_a_up, gla_a_b, gla_o_g, ca_q_g, ca_k_g, mem_norm_g, w_mem_kv, w_br_sb, w_br_gla, w_br_ca, w_out, ffn_norm_g, ff_w_gate_up, ff_w_down, moe_w_router, moe_w_gate_up, moe_w_down):
    batch, seq, d = x.shape
    depth = w_in.shape[0]
    n = batch * seq
    xf = x.reshape(n, d)
    memf = mem.reshape(-1, d)
    bf = lambda a: a.astype(BF16)

    for layer in range(depth):
        proj = norm_matmul(xf, mix_norm_g[layer], _slab_weights(w_in[layer]), tm=1024, tn=2304, slabs=True)
        mkv = norm_matmul(memf, mem_norm_g[layer], bf(w_mem_kv[layer]), tm=512, tn=1024, slabs=False)
        o_a = sb_attention(proj, sb_q_g[layer], sb_k_g[layer], batch=batch, seq=seq, tq=256, tk=128)
        o_b = gla(proj, gla_a_up[layer], gla_a_b[layer], gla_o_g[layer], batch=batch, seq=seq)
        xf = merge_branches(xf, o_a, o_b, proj, mkv, ca_q_g[layer], ca_k_g[layer],
                            bf(w_br_sb[layer]), bf(w_br_gla[layer]), bf(w_br_ca[layer]), bf(w_out[layer]),
                            batch=batch, seq=seq, tm=512)
        j = layer // 2
        if layer % 2 == 0:
            xf = ffn_dense(xf, ffn_norm_g[layer], bf(ff_w_gate_up[j]), bf(ff_w_down[j]), tm=512, tf=1408)
        else:
            route = router(xf, ffn_norm_g[layer], moe_w_router[j], tm=512)
            xf = ffn_moe(xf, ffn_norm_g[layer], route, bf(moe_w_gate_up[j]), bf(moe_w_down[j]), tm=512, tf=1792)
    return xf.reshape(batch, seq, d)
```

```python
import functools

import jax
import jax.numpy as jnp
from jax import lax
from jax.experimental import pallas as pl
from jax.experimental.pallas import tpu as pltpu

F32 = jnp.float32
BF16 = jnp.bfloat16

LANES = 128
EPS = 1e-6
LOG2E = 1.4426950408889634
LN2 = 0.6931471805599453
MASKED = 1e30
GATHER_UNROLL = 8
SB_STAGES = 4
SB_EXIT = -110.0
VMEM_LIMIT_BYTES = 56 * 1024 * 1024

D_MODEL = 1024
SB_HEADS, SB_HEAD_DIM = 8, 64
GLA_HEADS, GLA_DK, GLA_DV = 4, 64, 128
GLA_GATE_RANK, GLA_GATE_TAU, GLA_CHUNK = 16, 16.0, 64
CA_HEADS, CA_HEAD_DIM = 4, 128
N_BRANCH = 3
N_EXPERTS, TOP_K = 8, 2

SLAB_SB_Q, SLAB_SB_K, SLAB_SB_V = 0, 4, 8
SLAB_GLA_V, SLAB_GLA_R = 12, 16
SLAB_CA_Q = 20
SLAB_GATES = 24
SLAB_GLA_Q, SLAB_GLA_K, SLAB_GLA_A = 48, 50, 52
N_SLABS = 54


def _params(*semantics):
    return pltpu.CompilerParams(dimension_semantics=semantics, vmem_limit_bytes=VMEM_LIMIT_BYTES)


def _rms_rows(x, eps=EPS):
    return x * lax.rsqrt(jnp.mean(x * x, axis=-1, keepdims=True) + eps)


def _rms_half_lanes(x, eps=EPS):
    lane = lax.broadcasted_iota(jnp.int32, x.shape, 1)
    lo = lane < SB_HEAD_DIM
    sq = x * x
    s_lo = jnp.sum(jnp.where(lo, sq, 0.0), axis=-1, keepdims=True)
    s_hi = jnp.sum(jnp.where(lo, 0.0, sq), axis=-1, keepdims=True)
    ms = jnp.where(lo, s_lo, s_hi) * (1.0 / SB_HEAD_DIM)
    return x * lax.rsqrt(ms + eps)


def _split_bf16(x):
    hi = x.astype(BF16)
    lo = (x - hi.astype(F32)).astype(BF16)
    return hi, lo


def _norm_matmul_kernel(x_ref, g_ref, w_ref, o_ref, xn_ref, *, n_slab):
    @pl.when(pl.program_id(1) == 0)
    def _():
        xn_ref[...] = (_rms_rows(x_ref[...]) * g_ref[...]).astype(BF16)

    acc = jnp.dot(xn_ref[...], w_ref[...], preferred_element_type=F32)
    if n_slab is None:
        o_ref[...] = acc.astype(o_ref.dtype)
    else:
        for s in range(n_slab):
            o_ref[s] = acc[:, s * LANES:(s + 1) * LANES].astype(o_ref.dtype)


def norm_matmul(x, g, w, *, tm, tn, slabs):
    n, k = x.shape
    m = w.shape[1]
    tm = min(tm, n)
    tn = min(tn, m)
    assert n % tm == 0 and m % tn == 0 and tn % LANES == 0
    if slabs:
        n_slab = tn // LANES
        out_shape = jax.ShapeDtypeStruct((m // LANES, n, LANES), BF16)
        out_spec = pl.BlockSpec((n_slab, tm, LANES), lambda i, j: (j, i, 0))
    else:
        n_slab = None
        out_shape = jax.ShapeDtypeStruct((n, m), BF16)
        out_spec = pl.BlockSpec((tm, tn), lambda i, j: (i, j))
    return pl.pallas_call(
        functools.partial(_norm_matmul_kernel, n_slab=n_slab),
        grid=(n // tm, m // tn),
        in_specs=[
            pl.BlockSpec((tm, k), lambda i, j: (i, 0)),
            pl.BlockSpec((1, k), lambda i, j: (0, 0)),
            pl.BlockSpec((k, tn), lambda i, j: (0, j)),
        ],
        out_specs=out_spec,
        out_shape=out_shape,
        scratch_shapes=[pltpu.VMEM((tm, k), BF16)],
        compiler_params=_params("parallel", "arbitrary"),
        name="norm_matmul",
    )(x, g.reshape(1, k), w)


def _softplus(z):
    return jnp.maximum(z, 0.0) + jnp.log(1.0 + jnp.exp(-jnp.abs(z)))


def _sb_items(nq, n_diag):
    qi, kj = [], []
    for i in range(nq):
        for j in reversed(range((i + 1) * n_diag)):
            qi.append(i)
            kj.append(j)
    n_items = len(qi)
    tile_start = [i * (i + 1) // 2 * n_diag for i in range(nq + 1)]
    assert tile_start[nq] == n_items
    qi += [0] * n_diag * SB_STAGES
    kj += list(reversed(range(n_diag))) * SB_STAGES
    return n_items, qi, kj, tile_start


def _sb_kernel(qi_tab, kj_tab, ts_tab, q_ref, k_ref, v_ref, qg_ref, kg_ref, o_ref,
               qh_ref, kn_ref, z_ref, sp_ref, bt_ref, w_ref, rs_ref, *, tq, tk, seq, n_items):
    n_diag = tq // tk
    rows = min(512, seq)
    lane = lax.broadcasted_iota(jnp.int32, (rows, LANES), 1)
    scale = SB_HEAD_DIM ** -0.5

    def norm_chunk(c, carry):
        r0 = pl.multiple_of(c * rows, rows)
        kk = k_ref[0, pl.ds(r0, rows), :].astype(F32)
        kn_ref[pl.ds(r0, rows), :] = (_rms_half_lanes(kk) * kg_ref[...]).astype(BF16)
        qn = _rms_half_lanes(q_ref[0, pl.ds(r0, rows), :].astype(F32)) * (qg_ref[...] * scale)
        qh_ref[0, pl.ds(r0, rows), :] = jnp.where(lane < SB_HEAD_DIM, qn, 0.0).astype(BF16)
        qh_ref[1, pl.ds(r0, rows), :] = jnp.where(lane >= SB_HEAD_DIM, qn, 0.0).astype(BF16)
        return carry
    lax.fori_loop(0, seq // rows, norm_chunk, 0)

    z_ref[...] = jnp.zeros_like(z_ref)
    sp_ref[...] = jnp.zeros_like(sp_ref)
    bt_ref[...] = jnp.zeros_like(bt_ref)
    w_ref[...] = jnp.zeros_like(w_ref)
    rs_ref[...] = jnp.zeros_like(rs_ref)
    o_ref[...] = jnp.zeros_like(o_ref)

    rj = lax.broadcasted_iota(jnp.int32, (tk, tk), 0)
    cs = lax.broadcasted_iota(jnp.int32, (tk, tk), 1)
    u_strict = jnp.where(rj > cs, -1.0, 0.0).astype(BF16)
    rel = (lax.broadcasted_iota(jnp.int32, (tq, tk), 1) - lax.broadcasted_iota(jnp.int32, (tq, tk), 0))
    out_lane = lax.broadcasted_iota(jnp.int32, (tq, LANES), 1) < SB_HEAD_DIM

    def item(m):
        return qi_tab[m], kj_tab[m]

    def step(slot, ids, carries, accs):
        prev = 1 - slot
        qi1, kj1 = item(ids[0])
        kb = kn_ref[pl.ds(pl.multiple_of(kj1 * tk, tk), tk), :]
        for h in range(2):
            qh = qh_ref[h, pl.ds(pl.multiple_of(qi1 * tq, tq), tq), :]
            z_ref[slot, h] = lax.dot_general(qh, kb, (((1,), (1,)), ((), ())), preferred_element_type=F32)
        qi2, kj2 = item(ids[1])
        keep = rel < (qi2 * tq - kj2 * tk)
        for h in range(2):
            z = jnp.where(keep, z_ref[prev, h], -MASKED)
            e = jnp.exp2(jnp.abs(z) * (-LOG2E))
            r = pl.reciprocal(1.0 + e)
            sp =jnp.maximum(z, 0.0) - jnp.log(r)
            beta = jnp.where(z >= 0.0, r, e * r)
            sp_ref[slot, h] = sp.astype(BF16)
            bt_ref[slot, h] = beta.astype(BF16)
            rs_ref[slot, h] = jnp.broadcast_to(jnp.sum(sp, axis=1, keepdims=True), (tq, LANES))
        qi3, kj3 = item(ids[2])
        first3 = kj3 == (qi3 + 1) * n_diag - 1
        new_carries = []
        for h in range(2):
            t = jnp.dot(sp_ref[prev, h], u_strict, preferred_element_type=F32)
            c_in = jnp.where(first3, 0.0, carries[h])
            c_wide = jnp.concatenate([c_in] * (tk // LANES), axis=1)
            w = jnp.exp2((t + c_wide) * LOG2E) * bt_ref[prev, h].astype(F32)
            w_ref[slot, h] = w.astype(BF16)
            new_carries.append(c_in - rs_ref[prev, h])
        qi4, kj4 = item(ids[3])
        first4 = kj4 == (qi4 + 1) * n_diag - 1
        vb = v_ref[0, pl.ds(pl.multiple_of(kj4 * tk, tk), tk), :]
        new_accs = []
        for h in range(2):
            pv = jnp.dot(w_ref[prev, h], vb, preferred_element_type=F32)
            new_accs.append(jnp.where(first4, 0.0, accs[h]) + pv)
        out_rows = pl.ds(pl.multiple_of(qi4 * tq, tq), tq)
        result = jnp.where(out_lane, new_accs[0], new_accs[1]).astype(o_ref.dtype)
        o_ref[0, out_rows, :] = jnp.where(ids[3] < n_items, result, o_ref[0, out_rows, :])
        return new_carries, new_accs

    def advance(slot, st):
        ptr, in2, in3, in4 = st[:4]
        carries, accs = [st[4], st[5]], [st[6], st[7]]
        in1 = jnp.minimum(ptr, n_items)
        carries, accs = step(slot, (in1, in2, in3, in4), carries, accs)
        nxt = jnp.minimum(ptr + 1, n_items)
        tile3 = qi_tab[in3]
        done = jnp.max(jnp.maximum(carries[0], carries[1])) < SB_EXIT
        jump = done & (in3 < n_items) & (nxt < n_items) & (qi_tab[nxt] == tile3)
        ptr = jnp.where(jump, ts_tab[tile3 + 1], nxt)
        return (ptr, in1, in2, in3, carries[0], carries[1], accs[0], accs[1])

    def body(st):
        return advance(1, advance(0, st))

    def pending(st):
        ptr, in2, in3, in4 = st[:4]
        return (ptr < n_items) | (in2 < n_items) | (in3 < n_items) | (in4 < n_items)

    zeros = jnp.zeros((tq, LANES), F32)
    filler = jnp.int32(n_items)
    lax.while_loop(pending, body, (jnp.int32(0), filler, filler, filler, zeros, zeros, zeros, zeros))


def sb_attention(proj, q_gain, k_gain, *, batch, seq, tq, tk):
    tq = min(tq, seq)
    tk = min(tk, tq)
    assert seq % tq == 0 and tq % tk == 0
    n_pairs = SB_HEADS // 2
    n_items, qi, kj, tile_start = _sb_items(seq // tq, tq // tk)
    g2 = lambda g: jnp.concatenate([g, g]).reshape(1, LANES).astype(F32)
    seq_blk = lambda s0: pl.BlockSpec((1, seq, LANES), lambda b, p, qt, kt, ts: (s0 + p, b, 0))
    stage = lambda dt: pltpu.VMEM((2, 2, tq, tk), dt)
    return pl.pallas_call(
        functools.partial(_sb_kernel, tq=tq, tk=tk, seq=seq, n_items=n_items),
        grid_spec=pltpu.PrefetchScalarGridSpec(
            num_scalar_prefetch=3,
            grid=(batch, n_pairs),
            in_specs=[
                seq_blk(SLAB_SB_Q), seq_blk(SLAB_SB_K), seq_blk(SLAB_SB_V),
                pl.BlockSpec((1, LANES), lambda b, p, qt, kt, ts: (0, 0)),
                pl.BlockSpec((1, LANES), lambda b, p, qt, kt, ts: (0, 0)),
            ],
            out_specs=pl.BlockSpec((1, seq, LANES), lambda b, p, qt, kt, ts: (p, b, 0)),
            scratch_shapes=[
                pltpu.VMEM((2, seq, LANES), BF16),
                pltpu.VMEM((seq, LANES), BF16),
                stage(F32), stage(BF16), stage(BF16), stage(BF16),
                pltpu.VMEM((2, 2, tq, LANES), F32),
            ],
        ),
        out_shape=jax.ShapeDtypeStruct((n_pairs, batch * seq, LANES), BF16),
        compiler_params=_params("parallel", "parallel"),
        name="sb_attention",
    )(jnp.asarray(qi, jnp.int32), jnp.asarray(kj, jnp.int32), jnp.asarray(tile_start, jnp.int32), proj, proj, proj, g2(q_gain), g2(k_gain))


GLA_ROWS = 256
GLA_BLOCKS_PER_STEP = 2
GLA_KW = GLA_HEADS * GLA_DK


def _gla_kernel(q_ref, k_ref, v_ref, a_ref, r_ref, aup_ref, ab_ref, og_ref, o_ref, state_ref):
    @pl.when(pl.program_id(1) == 0)
    def _():
        state_ref[...] = jnp.zeros_like(state_ref)

    state = state_ref[...]
    for blk in range(o_ref.shape[1] // GLA_ROWS):
        state = _gla_block(blk * GLA_ROWS, state, q_ref, k_ref, v_ref, a_ref, r_ref,
                           aup_ref, ab_ref, og_ref, o_ref)
    state_ref[...] = state


def _gla_block(b0, state, q_ref, k_ref, v_ref, a_ref, r_ref, aup_ref, ab_ref, og_ref, o_ref):
    rows = GLA_ROWS
    n_chunks = rows // GLA_CHUNK
    blk = pl.ds(b0, rows)

    pre = jnp.dot(a_ref[0, blk, :], aup_ref[...], preferred_element_type=F32) + ab_ref[...]
    log_a = -_softplus(-pre) * (1.0 / GLA_GATE_TAU)
    la_hi, la_lo = _split_bf16(log_a)

    ri = lax.broadcasted_iota(jnp.int32, (rows, rows), 0)
    ci = lax.broadcasted_iota(jnp.int32, (rows, rows), 1)
    same_chunk = (ri // GLA_CHUNK) == (ci // GLA_CHUNK)
    tri = same_chunk & (ri >= ci)
    tri_b = jnp.where(tri, 1.0, 0.0).astype(BF16)
    ones_b = jnp.where(same_chunk, 1.0, 0.0).astype(BF16)
    cum = (jnp.dot(tri_b, la_hi, preferred_element_type=F32)
           + jnp.dot(tri_b, la_lo, preferred_element_type=F32))
    cum_last = (jnp.dot(ones_b, la_hi, preferred_element_type=F32)
                + jnp.dot(ones_b, la_lo, preferred_element_type=F32))

    q_all = jnp.concatenate([q_ref[0, blk, :], q_ref[1, blk, :]], axis=-1).astype(F32)
    k_all = jnp.concatenate([k_ref[0, blk, :], k_ref[1, blk, :]], axis=-1).astype(F32)
    q_dec = q_all * (GLA_DK ** -0.5) * jnp.exp(cum)
    k_inv = (k_all * jnp.exp(-cum)).astype(BF16)
    k_end = (k_all * jnp.exp(cum_last - cum)).astype(BF16)
    v_heads = [v_ref[h, blk, :] for h in range(GLA_HEADS)]
    v_all = jnp.concatenate(v_heads, axis=-1)

    lane = lax.broadcasted_iota(jnp.int32, (rows, GLA_KW), 1)
    q_heads = [jnp.where((lane // GLA_DK) == h, q_dec, 0.0).astype(BF16) for h in range(GLA_HEADS)]

    o_heads = []
    for h in range(GLA_HEADS):
        sc = lax.dot_general(q_heads[h], k_inv, (((1,), (1,)), ((), ())), preferred_element_type=F32)
        sc = jnp.where(tri, sc, 0.0).astype(BF16)
        o_heads.append(jnp.dot(sc, v_heads[h], preferred_element_type=F32))

    ones_cols = jnp.ones((GLA_CHUNK, LANES), BF16)
    inter =[[] for _ in range(GLA_HEADS)]
    for c in range(n_chunks):
        r0, r1 = c * GLA_CHUNK, (c + 1) * GLA_CHUNK
        state_b = state.astype(BF16)
        for h in range(GLA_HEADS):
            inter[h].append(jnp.dot(q_heads[h][r0:r1], state_b, preferred_element_type=F32))
        kv = lax.dot_general(k_end[r0:r1], v_all[r0:r1], (((0,), (0,)), ((), ())),
                             preferred_element_type=F32)
        kv_sel = jnp.concatenate(
            [kv[h * GLA_DK:(h + 1) * GLA_DK, h * GLA_DV:(h + 1) * GLA_DV] for h in range(GLA_HEADS)], axis=0)
        tot = (lax.dot_general(la_hi[r0:r1], ones_cols, (((0,), (0,)), ((), ())), preferred_element_type=F32)
               + lax.dot_general(la_lo[r0:r1], ones_cols, (((0,), (0,)), ((), ())), preferred_element_type=F32))
        state = jnp.exp(tot) * state + kv_sel

    for h in range(GLA_HEADS):
        o = o_heads[h] + jnp.concatenate(inter[h], axis=0)
        o = _rms_rows(o) * og_ref[...]
        r = r_ref[h, blk, :].astype(F32)
        o_ref[h, blk, :] = (o * (r * jax.nn.sigmoid(r))).astype(o_ref.dtype)
    return state


def gla(proj, a_up, a_b, o_gain, *, batch, seq):
    rows = GLA_ROWS * GLA_BLOCKS_PER_STEP
    if seq % rows:
        rows = GLA_ROWS
    assert seq % rows == 0
    nb = seq // rows
    a_up_p = jnp.zeros((LANES, GLA_KW), BF16).at[:GLA_GATE_RANK].set(a_up.astype(BF16))
    row_blk = lambda n, s0: pl.BlockSpec((n, rows, LANES), lambda b, i: (s0 // n, b * nb + i, 0))
    return pl.pallas_call(
        _gla_kernel,
        grid=(batch, nb),
        in_specs=[
            row_blk(2, SLAB_GLA_Q), row_blk(2, SLAB_GLA_K), row_blk(4, SLAB_GLA_V),
            row_blk(1, SLAB_GLA_A), row_blk(4, SLAB_GLA_R),
            pl.BlockSpec((LANES, GLA_KW), lambda b, i: (0, 0)),
            pl.BlockSpec((1, GLA_KW), lambda b, i: (0, 0)),
            pl.BlockSpec((1, GLA_DV), lambda b, i: (0, 0)),
        ],
        out_specs=pl.BlockSpec((GLA_HEADS, rows, LANES), lambda b, i: (0, b * nb + i, 0)),
        out_shape=jax.ShapeDtypeStruct((GLA_HEADS, batch * seq, LANES), BF16),
        scratch_shapes=[pltpu.VMEM((GLA_KW, GLA_DV), F32)],
        compiler_params=_params("parallel", "arbitrary"),
        name="gla",
    )(proj, proj, proj, proj, proj, a_up_p,
      a_b.reshape(1, GLA_KW).astype(F32), o_gain.reshape(1, GLA_DV).astype(F32))


def _merge_kernel(x_ref, oa_ref, ob_ref, cq_ref, gate_ref, mkv_ref, cqg_ref, ckg_ref,
                  wsb_ref, wgla_ref, wca_ref, wout_ref, o_ref):
    ca_w = CA_HEADS * CA_HEAD_DIM
    oc = []
    for h in range(CA_HEADS):
        q = _rms_rows(cq_ref[h].astype(F32)) * (cqg_ref[...] * CA_HEAD_DIM ** -0.5)
        kk = _rms_rows(mkv_ref[:, h * CA_HEAD_DIM:(h + 1) * CA_HEAD_DIM].astype(F32)) * ckg_ref[...]
        s = lax.dot_general(q.astype(BF16), kk.astype(BF16), (((1,), (1,)), ((), ())),
                            preferred_element_type=F32)
        e = jnp.exp(s - jnp.max(s, axis=-1, keepdims=True))
        p = e * (1.0 / jnp.sum(e, axis=-1, keepdims=True))
        vv = mkv_ref[:, ca_w + h * CA_HEAD_DIM: ca_w + (h + 1) * CA_HEAD_DIM]
        oc.append(jnp.dot(p.astype(BF16), vv, preferred_element_type=F32).astype(BF16))
    o_c = jnp.concatenate(oc, axis=-1)
    o_a = jnp.concatenate([oa_ref[s] for s in range(4)], axis=-1)
    o_b = jnp.concatenate([ob_ref[s] for s in range(4)], axis=-1)

    def gate(b):
        g = jnp.concatenate([gate_ref[b * 8 + s] for s in range(8)], axis=-1).astype(F32)
        return jax.nn.sigmoid(g)

    merged = gate(0) * jnp.dot(o_a, wsb_ref[...], preferred_element_type=F32)
    merged += gate(1) * jnp.dot(o_b, wgla_ref[...], preferred_element_type=F32)
    merged += gate(2) * jnp.dot(o_c, wca_ref[...], preferred_element_type=F32)
    o_ref[...] = x_ref[...] + jnp.dot(merged.astype(BF16), wout_ref[...], preferred_element_type=F32)


def merge_branches(x, o_a, o_b, proj, mkv, cq_gain, ck_gain, w_sb, w_gla, w_ca, w_out, *, batch, seq, tm):
    n, d = x.shape
    tm = min(tm, seq)
    assert seq % tm == 0
    nt = seq // tm
    mem_len = mkv.shape[0] // batch
    full = lambda a: pl.BlockSpec(a.shape, lambda i: (0,) * a.ndim)
    cqg = cq_gain.reshape(1, CA_HEAD_DIM).astype(F32)
    ckg = ck_gain.reshape(1, CA_HEAD_DIM).astype(F32)
    return pl.pallas_call(
        _merge_kernel,
        grid=(n // tm,),
        in_specs=[
            pl.BlockSpec((tm, d), lambda i: (i, 0)),
            pl.BlockSpec((4, tm, LANES), lambda i: (0, i, 0)),
            pl.BlockSpec((4, tm, LANES), lambda i: (0, i, 0)),
            pl.BlockSpec((4, tm, LANES), lambda i: (SLAB_CA_Q // 4, i, 0)),
            pl.BlockSpec((24, tm, LANES), lambda i: (SLAB_GATES // 24, i, 0)),
            pl.BlockSpec((mem_len, mkv.shape[1]), lambda i: (i // nt, 0)),
            full(cqg), full(ckg), full(w_sb), full(w_gla), full(w_ca), full(w_out),
        ],
        out_specs=pl.BlockSpec((tm, d), lambda i: (i, 0)),
        out_shape=jax.ShapeDtypeStruct((n, d), F32),
        compiler_params=_params("parallel"),
        name="merge_branches",
    )(x, o_a, o_b, proj, proj, mkv,
      cqg, ckg, w_sb, w_gla, w_ca, w_out)


def _ffn_kernel(x_ref, g_ref, wg_ref, wu_ref, wd_ref, o_ref, hn_ref, acc_ref):
    f = pl.program_id(1)

    @pl.when(f == 0)
    def _():
        hn_ref[...] = (_rms_rows(x_ref[...]) * g_ref[...]).astype(BF16)
        acc_ref[...] = jnp.zeros_like(acc_ref)

    h = hn_ref[...]
    gt = jnp.dot(h, wg_ref[...], preferred_element_type=F32)
    up = jnp.dot(h, wu_ref[...], preferred_element_type=F32)
    act = (gt * jax.nn.sigmoid(gt) * up).astype(BF16)
    acc_ref[...] += jnp.dot(act, wd_ref[...], preferred_element_type=F32)

    @pl.when(f == pl.num_programs(1) - 1)
    def _():
        o_ref[...] = x_ref[...] + acc_ref[...]


def ffn_dense(x, g, w_gate_up, w_down, *, tm, tf):
    n, d = x.shape
    d_ff = w_down.shape[0]
    tm = min(tm, n)
    assert n % tm == 0 and d_ff % tf == 0 and tf % LANES == 0
    nf = d_ff // tf
    return pl.pallas_call(
        _ffn_kernel,
        grid=(n // tm, nf),
        in_specs=[
            pl.BlockSpec((tm, d), lambda i, f: (i, 0)),
            pl.BlockSpec((1, d), lambda i, f: (0, 0)),
            pl.BlockSpec((d, tf), lambda i, f: (0, f)),
            pl.BlockSpec((d, tf), lambda i, f: (0, nf + f)),
            pl.BlockSpec((tf, d), lambda i, f: (f, 0)),
        ],
        out_specs=pl.BlockSpec((tm, d), lambda i, f: (i, 0)),
        out_shape=jax.ShapeDtypeStruct((n, d), F32),
        scratch_shapes=[pltpu.VMEM((tm, d), BF16), pltpu.VMEM((tm, d), F32)],
        compiler_params=_params("parallel", "arbitrary"),
        name="ffn_dense",
    )(x, g.reshape(1, d), w_gate_up, w_gate_up, w_down)


def _router_kernel(x_ref, g_ref, wr_ref, c_ref):
    hn = _rms_rows(x_ref[...]) * g_ref[...]
    logits = jnp.dot(hn, wr_ref[...], preferred_element_type=F32, precision=lax.Precision.HIGHEST)
    lane = lax.broadcasted_iota(jnp.int32, logits.shape, 1).astype(F32)
    neg = jnp.float32(-jnp.inf)
    logits = jnp.where(lane < N_EXPERTS, logits, neg)
    m1 = jnp.max(logits, axis=-1, keepdims=True)
    i1 = jnp.min(jnp.where(logits == m1, lane, float(LANES)), axis=-1, keepdims=True)
    rest = jnp.where(lane == i1, neg, logits)
    m2 = jnp.max(rest, axis=-1, keepdims=True)
    i2 = jnp.min(jnp.where(rest == m2, lane, float(LANES)), axis=-1, keepdims=True)
    e2 = jnp.exp(m2 - m1)
    inv = 1.0 / (1.0 + e2)
    c_ref[...] = (jnp.where(lane == 0.0, inv, 0.0) + jnp.where(lane == 1.0, e2 * inv, 0.0)
                  + jnp.where(lane == 2.0, i1, 0.0) + jnp.where(lane == 3.0, i2, 0.0))


def router(x, g, w_router, *, tm):
    n, d = x.shape
    tm = min(tm, n)
    wr = jnp.zeros((d, LANES), F32).at[:, :N_EXPERTS].set(w_router)
    return pl.pallas_call(
        _router_kernel,
        grid=(n // tm,),
        in_specs=[
            pl.BlockSpec((tm, d), lambda i: (i, 0)),
            pl.BlockSpec((1, d), lambda i: (0, 0)),
            pl.BlockSpec((d, LANES), lambda i: (0, 0)),
        ],
        out_specs=pl.BlockSpec((tm, LANES), lambda i: (i, 0)),
        out_shape=jax.ShapeDtypeStruct((n, LANES), F32),
        compiler_params=_params("parallel"),
        name="router",
    )(x, g.reshape(1, d), wr)


def _row_copy(src_hbm, row, dst_vmem, r, sem):
    return pltpu.make_async_copy(src_hbm.at[pl.ds(row, 1)], dst_vmem.at[pl.ds(r, 1)], sem)


def _expert_ffn_kernel(te_ref, nu_ref, cur_ref, nxt_ref, x_hbm, g_ref, wg_ref, wu_ref, wd_ref, o_ref,
                       xbuf_ref, sem, hn_ref, acc_ref, *, nf):
    i = pl.program_id(0)
    f = pl.program_id(1)
    rows = o_ref.shape[0]
    used = i < nu_ref[0]
    slot = i % 2

    def normalise(s):
        hn_ref[s] = (_rms_rows(xbuf_ref[...]) * g_ref[...]).astype(BF16)

    @pl.when((i == 0) & (f == 0))
    def _():
        def start(r, carry):
            _row_copy(x_hbm, cur_ref[0, 0, r], xbuf_ref, r, sem).start()
            return carry
        lax.fori_loop(0, rows, start, 0, unroll=GATHER_UNROLL)

        def wait(r, carry):
            _row_copy(x_hbm, 0, xbuf_ref, r, sem).wait()
            return carry
        lax.fori_loop(0, rows, wait, 0, unroll=GATHER_UNROLL)
        normalise(0)

    def ffn_step(first, last):
        if first:
            for r in range(rows):
                _row_copy(x_hbm, nxt_ref[0, 0, r], xbuf_ref, r, sem).start()
        h = hn_ref[slot]
        gt = jnp.dot(h, wg_ref[0], preferred_element_type=F32)
        up = jnp.dot(h, wu_ref[0], preferred_element_type=F32)
        act = (gt * jax.nn.sigmoid(gt) * up).astype(BF16)
        part = jnp.dot(act, wd_ref[0], preferred_element_type=F32)
        total = part if first else acc_ref[...] + part
        if last:
            o_ref[...] = total
            for r in range(rows):
                _row_copy(x_hbm, 0, xbuf_ref, r, sem).wait()
            normalise(1 - slot)
        else:
            acc_ref[...] = total

    if nf == 1:
        pl.when(used)(lambda: ffn_step(True, True))
    else:
        pl.when(used & (f == 0))(lambda: ffn_step(True, False))
        pl.when(used & (f == nf - 1))(lambda: ffn_step(False, True))
        if nf > 2:
            pl.when(used & (f > 0) & (f < nf - 1))(lambda: ffn_step(False, False))

    @pl.when(jnp.logical_not(used) & (f == nf - 1))
    def _():
        o_ref[...] = jnp.zeros_like(o_ref)


def expert_ffn(x, src, g, tile_expert, n_used, w_gate_up, w_down, *, tm, tf):
    rows = src.shape[0]
    d = x.shape[1]
    d_ff = w_down.shape[1]
    assert rows % tm == 0 and d_ff % tf == 0 and tf % LANES == 0
    nf = d_ff // tf
    n_tiles = rows // tm
    src = src.reshape(n_tiles, 1, tm)
    return pl.pallas_call(
        functools.partial(_expert_ffn_kernel, nf=nf),
        grid_spec=pltpu.PrefetchScalarGridSpec(
            num_scalar_prefetch=2,
            grid=(n_tiles, nf),
            in_specs=[
                pl.BlockSpec((1, 1, tm), lambda i, f, te, nu: (i, 0, 0), memory_space=pltpu.SMEM),
                pl.BlockSpec((1, 1, tm), lambda i, f, te, nu: (jnp.minimum(i + 1, n_tiles - 1), 0, 0),
                             memory_space=pltpu.SMEM),
                pl.BlockSpec(memory_space=pl.ANY),
                pl.BlockSpec((1, d), lambda i, f, te, nu: (0, 0)),
                pl.BlockSpec((1, d, tf), lambda i, f, te, nu: (te[i], 0, f)),
                pl.BlockSpec((1, d, tf), lambda i, f, te, nu: (te[i], 0, nf + f)),
                pl.BlockSpec((1, tf, d), lambda i, f, te, nu: (te[i], f, 0)),
            ],
            out_specs=pl.BlockSpec((tm, d), lambda i, f, te, nu: (i, 0)),
            scratch_shapes=[pltpu.VMEM((tm, d), F32), pltpu.SemaphoreType.DMA,
                            pltpu.VMEM((2, tm, d), BF16), pltpu.VMEM((tm, d), F32)],
        ),
        out_shape=jax.ShapeDtypeStruct((rows, d), F32),
        compiler_params=_params("arbitrary", "arbitrary"),
        name="expert_ffn",
    )(tile_expert, n_used, src, src, x, g.reshape(1, d), w_gate_up, w_gate_up, w_down)


def _combine_kernel(p1_ref, p2_ref, n1_ref, n2_ref, x_ref, rw_ref, ys_hbm, o_ref, a_ref, b_ref, sem):
    i = pl.program_id(0)
    rows = o_ref.shape[0]
    slot = i % 2

    def issue(q1_ref, q2_ref, s):
        def one(r, carry):
            _row_copy(ys_hbm, q1_ref[0, 0, r], a_ref.at[s], r, sem.at[0, s]).start()
            _row_copy(ys_hbm, q2_ref[0, 0, r], b_ref.at[s], r, sem.at[1, s]).start()
            return carry
        lax.fori_loop(0, rows, one, 0, unroll=GATHER_UNROLL)

    @pl.when(i == 0)
    def _():
        issue(p1_ref, p2_ref, 0)

    @pl.when(i + 1 < pl.num_programs(0))
    def _():
        issue(n1_ref, n2_ref, 1 - slot)

    def drain(r, carry):
        _row_copy(ys_hbm, 0, a_ref.at[slot], r, sem.at[0, slot]).wait()
        _row_copy(ys_hbm, 0, b_ref.at[slot], r, sem.at[1, slot]).wait()
        return carry
    lax.fori_loop(0, rows, drain, 0, unroll=GATHER_UNROLL)

    rw = rw_ref[...]
    o_ref[...] = x_ref[...] + (rw[:, 0:1] * a_ref[slot] + rw[:, 1:2] * b_ref[slot])


def moe_combine(x, route, ys, pos1, pos2, *, tc):
    n, d = x.shape
    assert n % tc == 0
    nt = n // tc
    idx_spec = pl.BlockSpec((1, 1, tc), lambda i: (i, 0, 0), memory_space=pltpu.SMEM)
    nxt_spec = pl.BlockSpec((1, 1, tc), lambda i: (jnp.minimum(i + 1, nt - 1), 0, 0), memory_space=pltpu.SMEM)
    pos1 = pos1.reshape(nt, 1, tc)
    pos2 = pos2.reshape(nt, 1, tc)
    return pl.pallas_call(
        _combine_kernel,
        grid=(nt,),
        in_specs=[
            idx_spec, idx_spec, nxt_spec, nxt_spec,
            pl.BlockSpec((tc, d), lambda i: (i, 0)),
            pl.BlockSpec((tc, LANES), lambda i: (i, 0)),
            pl.BlockSpec(memory_space=pl.ANY),
        ],
        out_specs=pl.BlockSpec((tc, d), lambda i: (i, 0)),
        out_shape=jax.ShapeDtypeStruct((n, d), F32),
        scratch_shapes=[pltpu.VMEM((2, tc, d), F32), pltpu.VMEM((2, tc, d), F32),
                        pltpu.SemaphoreType.DMA((2, 2))],
        compiler_params=_params("arbitrary"),
        name="moe_combine",
    )(pos1, pos2, pos1, pos2, x, route, ys)


def _dispatch_plan(route, *, tm):
    n = route.shape[0]
    ids = route[:, 2:4].astype(jnp.int32)
    e_flat = ids.T.reshape(-1)
    onehot = (e_flat[:, None] == jnp.arange(N_EXPERTS)[None, :]).astype(jnp.int32)
    running = jnp.cumsum(onehot, axis=0)
    rank = jnp.sum(running * onehot, axis=1) - 1
    counts = running[-1]
    padded = ((counts + tm - 1) // tm) * tm
    ends = jnp.cumsum(padded)
    starts = ends - padded
    pos = starts[e_flat] + rank
    rows = TOP_K * n + N_EXPERTS * tm
    tok = jnp.tile(jnp.arange(n, dtype=jnp.int32), TOP_K)
    src = jnp.zeros((rows,), jnp.int32).at[pos].set(tok)
    tile_start = jnp.arange(rows // tm, dtype=jnp.int32) * tm
    tile_expert = jnp.minimum(jnp.sum(tile_start[:, None] >= ends[None, :], axis=1), N_EXPERTS - 1)
    n_used = (ends[-1] // tm).reshape(1).astype(jnp.int32)
    return src, tile_expert.astype(jnp.int32), n_used, pos[:n].astype(jnp.int32), pos[n:].astype(jnp.int32)


def ffn_moe(x, g, route, w_gate_up, w_down, *, tm, tf):
    src, tile_expert, n_used, pos1, pos2 = _dispatch_plan(route, tm=tm)
    ys = expert_ffn(x, src, g, tile_expert, n_used, w_gate_up, w_down, tm=tm, tf=tf)
    return moe_combine(x, route, ys, pos1, pos2, tc=min(256, x.shape[0]))


def _slab_weights(w_in):
    d = w_in.shape[0]
    sbw = SB_HEADS * SB_HEAD_DIM
    glv = GLA_HEADS * GLA_DV
    caw = CA_HEADS * CA_HEAD_DIM
    sizes = (sbw, sbw, sbw, GLA_KW, GLA_KW, glv, GLA_GATE_RANK, glv, caw, N_BRANCH * D_MODEL)
    parts, off = [], 0
    for s in sizes:
        parts.append(w_in[:, off:off + s])
        off += s
    sb_q, sb_k, sb_v, g_q, g_k, g_v, g_a, g_r, ca_q, gates = parts
    pad = jnp.zeros((d, 2 * LANES - GLA_GATE_RANK), w_in.dtype)
    cols = [sb_q, sb_k, sb_v, g_v, g_r, ca_q, gates, g_q, g_k, g_a, pad]
    w = jnp.concatenate([c.astype(BF16) for c in cols], axis=1)
    assert w.shape[1] == N_SLABS * LANES
    return w


def kernel(x, mem, mix_norm_g, w_in, sb_q_g, sb_k_g, gla_a_up, gla_a_b, gla_o_g, ca_q_g, ca_k_g, mem_norm_g, w_mem_kv, w_br_sb, w_br_gla, w_br_ca, w_out, ffn_norm_g, ff_w_gate_up, ff_w_down, moe_w_router, moe_w_gate_up, moe_w_down):
    batch, seq, d = x.shape
    depth = w_in.shape[0]
    n = batch * seq
    xf = x.reshape(n, d)
    memf = mem.reshape(-1, d)
    bf = lambda a: a.astype(BF16)

    for layer in range(depth):
        proj = norm_matmul(xf, mix_norm_g[layer], _slab_weights(w_in[layer]), tm=1024, tn=2304, slabs=True)
        mkv = norm_matmul(memf, mem_norm_g[layer], bf(w_mem_kv[layer]), tm=512, tn=1024, slabs=False)
        o_a = sb_attention(proj, sb_q_g[layer], sb_k_g[layer], batch=batch, seq=seq, tq=256, tk=128)
        o_b = gla(proj, gla_a_up[layer], gla_a_b[layer], gla_o_g[layer], batch=batch, seq=seq)
        xf = merge_branches(xf, o_a, o_b, proj, mkv, ca_q_g[layer], ca_k_g[layer],
                            bf(w_br_sb[layer]), bf(w_br_gla[layer]), bf(w_br_ca[layer]), bf(w_out[layer]),
                            batch=batch, seq=seq, tm=1024)
        j = layer // 2
        if layer % 2 == 0:
            xf = ffn_dense(xf, ffn_norm_g[layer], bf(ff_w_gate_up[j]), bf(ff_w_down[j]), tm=1024, tf=1408)
        else:
            route = router(xf, ffn_norm_g[layer], moe_w_router[j], tm=512)
            xf = ffn_moe(xf, ffn_norm_g[layer], route, bf(moe_w_gate_up[j]), bf(moe_w_down[j]), tm=512, tf=1792)
    return xf.reshape(batch, seq, d)
```

```python
import functools

import jax
import jax.numpy as jnp
from jax import lax
from jax.experimental import pallas as pl
from jax.experimental.pallas import tpu as pltpu

F32 = jnp.float32
BF16 = jnp.bfloat16

LANES = 128
EPS = 1e-6
LOG2E = 1.4426950408889634
MASKED = 1e30
GATHER_UNROLL = 8
SB_STAGES = 4
SB_EXIT = -110.0
VMEM_LIMIT_BYTES = 56 * 1024 * 1024

D_MODEL = 1024
SB_HEADS, SB_HEAD_DIM = 8, 64
GLA_HEADS, GLA_DK, GLA_DV = 4, 64, 128
GLA_GATE_RANK, GLA_GATE_TAU, GLA_CHUNK = 16, 16.0, 64
CA_HEADS, CA_HEAD_DIM = 4, 128
N_BRANCH = 3
N_EXPERTS, TOP_K = 8, 2

SLAB_SB_Q, SLAB_SB_K, SLAB_SB_V = 0, 4, 8
SLAB_GLA_V, SLAB_GLA_R = 12, 16
SLAB_CA_Q = 20
SLAB_GATES = 24
SLAB_GLA_Q, SLAB_GLA_K, SLAB_GLA_A = 48, 50, 52
N_SLABS = 54


TILES = {
    "in_proj": dict(tm=1024, tn=2304),
    "mem_kv": dict(tm=512, tn=1024),
    "sb": dict(tq=256, tk=128),
    "merge": dict(tm=1024),
    "ffn": dict(tm=1024, tf=1408),
    "router": dict(tm=512),
    "moe": dict(tm=512, tf=1792, tc=256),
}


def _params(*semantics):
    return pltpu.CompilerParams(dimension_semantics=semantics, vmem_limit_bytes=VMEM_LIMIT_BYTES)


def _rms_rows(x, eps=EPS):
    return x * lax.rsqrt(jnp.mean(x * x, axis=-1, keepdims=True) + eps)


def _rms_half_lanes(x, eps=EPS):
    lane = lax.broadcasted_iota(jnp.int32, x.shape, 1)
    lo = lane < SB_HEAD_DIM
    sq = x * x
    s_lo = jnp.sum(jnp.where(lo, sq, 0.0), axis=-1, keepdims=True)
    s_hi = jnp.sum(jnp.where(lo, 0.0, sq), axis=-1, keepdims=True)
    ms = jnp.where(lo, s_lo, s_hi) * (1.0 / SB_HEAD_DIM)
    return x * lax.rsqrt(ms + eps)


def _split_bf16(x):
    hi = x.astype(BF16)
    lo = (x - hi.astype(F32)).astype(BF16)
    return hi, lo


def _norm_matmul_kernel(x_ref, g_ref, w_ref, o_ref, xn_ref, *, n_slab):
    @pl.when(pl.program_id(1) == 0)
    def _():
        xn_ref[...] = (_rms_rows(x_ref[...]) * g_ref[...]).astype(BF16)

    acc = jnp.dot(xn_ref[...], w_ref[...], preferred_element_type=F32)
    if n_slab is None:
        o_ref[...] = acc.astype(o_ref.dtype)
    else:
        for s in range(n_slab):
            o_ref[s] = acc[:, s * LANES:(s + 1) * LANES].astype(o_ref.dtype)


def norm_matmul(x, g, w, *, tm, tn, slabs):
    n, k = x.shape
    m = w.shape[1]
    tm = min(tm, n)
    tn = min(tn, m)
    assert n % tm == 0 and m % tn == 0 and tn % LANES == 0
    if slabs:
        n_slab = tn // LANES
        out_shape = jax.ShapeDtypeStruct((m // LANES, n, LANES), BF16)
        out_spec = pl.BlockSpec((n_slab, tm, LANES), lambda i, j: (j, i, 0))
    else:
        n_slab = None
        out_shape = jax.ShapeDtypeStruct((n, m), BF16)
        out_spec = pl.BlockSpec((tm, tn), lambda i, j: (i, j))
    return pl.pallas_call(
        functools.partial(_norm_matmul_kernel, n_slab=n_slab),
        grid=(n // tm, m // tn),
        in_specs=[
            pl.BlockSpec((tm, k), lambda i, j: (i, 0)),
            pl.BlockSpec((1, k), lambda i, j: (0, 0)),
            pl.BlockSpec((k, tn), lambda i, j: (0, j)),
        ],
        out_specs=out_spec,
        out_shape=out_shape,
        scratch_shapes=[pltpu.VMEM((tm, k), BF16)],
        compiler_params=_params("parallel", "arbitrary"),
        name="norm_matmul",
    )(x, g.reshape(1, k), w)


def _softplus(z):
    return jnp.maximum(z, 0.0) + jnp.log(1.0 + jnp.exp(-jnp.abs(z)))


def _sb_items(nq, n_diag):
    qi, kj = [], []
    for i in range(nq):
        for j in reversed(range((i + 1) * n_diag)):
            qi.append(i)
            kj.append(j)
    n_items = len(qi)
    tile_start = [i * (i + 1) // 2 * n_diag for i in range(nq + 1)]
    assert tile_start[nq] == n_items
    qi += [0] * n_diag * SB_STAGES
    kj += list(reversed(range(n_diag))) * SB_STAGES
    return n_items, qi, kj, tile_start


def _sb_kernel(qi_tab, kj_tab, ts_tab, q_ref, k_ref, v_ref, qg_ref, kg_ref, o_ref,
               qh_ref, kn_ref, z_ref, sp_ref, bt_ref, w_ref, rs_ref, *, tq, tk, seq, n_items):
    n_diag = tq // tk
    rows = min(512, seq)
    lane = lax.broadcasted_iota(jnp.int32, (rows, LANES), 1)
    scale = SB_HEAD_DIM ** -0.5

    def norm_chunk(c, carry):
        r0 = pl.multiple_of(c * rows, rows)
        kk = k_ref[0, pl.ds(r0, rows), :].astype(F32)
        kn_ref[pl.ds(r0, rows), :] = (_rms_half_lanes(kk) * kg_ref[...]).astype(BF16)
        qn = _rms_half_lanes(q_ref[0, pl.ds(r0, rows), :].astype(F32)) * (qg_ref[...] * scale)
        qh_ref[0, pl.ds(r0, rows), :] = jnp.where(lane < SB_HEAD_DIM, qn, 0.0).astype(BF16)
        qh_ref[1, pl.ds(r0, rows), :] = jnp.where(lane >= SB_HEAD_DIM, qn, 0.0).astype(BF16)
        return carry
    lax.fori_loop(0, seq // rows, norm_chunk, 0)

    z_ref[...] = jnp.zeros_like(z_ref)
    sp_ref[...] = jnp.zeros_like(sp_ref)
    bt_ref[...] = jnp.zeros_like(bt_ref)
    w_ref[...] = jnp.zeros_like(w_ref)
    rs_ref[...] = jnp.zeros_like(rs_ref)
    o_ref[...] = jnp.zeros_like(o_ref)

    rj = lax.broadcasted_iota(jnp.int32, (tk, tk), 0)
    cs = lax.broadcasted_iota(jnp.int32, (tk, tk), 1)
    u_strict = jnp.where(rj > cs, -1.0, 0.0).astype(BF16)
    rel = (lax.broadcasted_iota(jnp.int32, (tq, tk), 1) - lax.broadcasted_iota(jnp.int32, (tq, tk), 0))
    out_lane = lax.broadcasted_iota(jnp.int32, (tq, LANES), 1) < SB_HEAD_DIM

    def item(m):
        return qi_tab[m], kj_tab[m]

    def step(slot, ids, carries, accs):
        prev = 1 - slot
        qi1, kj1 = item(ids[0])
        kb = kn_ref[pl.ds(pl.multiple_of(kj1 * tk, tk), tk), :]
        for h in range(2):
            qh = qh_ref[h, pl.ds(pl.multiple_of(qi1 * tq, tq), tq), :]
            z_ref[slot, h] = lax.dot_general(qh, kb, (((1,), (1,)), ((), ())), preferred_element_type=F32)
        qi2, kj2 = item(ids[1])
        keep = rel < (qi2 * tq - kj2 * tk)
        for h in range(2):
            z = jnp.where(keep, z_ref[prev, h], -MASKED)
            e = jnp.exp2(jnp.abs(z) * (-LOG2E))
            r = pl.reciprocal(1.0 + e)
            sp = jnp.maximum(z, 0.0) - jnp.log(r)
            beta = jnp.where(z >= 0.0, r, e * r)
            sp_ref[slot, h] = sp.astype(BF16)
            bt_ref[slot, h] = beta.astype(BF16)
            rs_ref[slot, h] = jnp.broadcast_to(jnp.sum(sp, axis=1, keepdims=True), (tq, LANES))
        qi3, kj3 = item(ids[2])
        first3 = kj3 == (qi3 + 1) * n_diag - 1
        new_carries = []
        for h in range(2):
            t = jnp.dot(sp_ref[prev, h], u_strict, preferred_element_type=F32)
            c_in = jnp.where(first3, 0.0, carries[h])
            c_wide = jnp.concatenate([c_in] * (tk // LANES), axis=1)
            w = jnp.exp2((t + c_wide) * LOG2E) * bt_ref[prev, h].astype(F32)
            w_ref[slot, h] = w.astype(BF16)
            new_carries.append(c_in - rs_ref[prev, h])
        qi4, kj4 = item(ids[3])
        first4 = kj4 == (qi4 + 1) * n_diag - 1
        vb = v_ref[0, pl.ds(pl.multiple_of(kj4 * tk, tk), tk), :]
        new_accs = []
        for h in range(2):
            pv = jnp.dot(w_ref[prev, h], vb, preferred_element_type=F32)
            new_accs.append(jnp.where(first4, 0.0, accs[h]) + pv)
        out_rows = pl.ds(pl.multiple_of(qi4 * tq, tq), tq)
        result = jnp.where(out_lane, new_accs[0], new_accs[1]).astype(o_ref.dtype)
        o_ref[0, out_rows, :] = jnp.where(ids[3] < n_items, result, o_ref[0, out_rows, :])
        return new_carries, new_accs

    def advance(slot, st):
        ptr, in2, in3, in4 = st[:4]
        carries, accs = [st[4], st[5]], [st[6], st[7]]
        in1 = jnp.minimum(ptr, n_items)
        carries, accs = step(slot, (in1, in2, in3, in4), carries, accs)
        nxt = jnp.minimum(ptr + 1, n_items)
        tile3 = qi_tab[in3]
        done = jnp.max(jnp.maximum(carries[0], carries[1])) < SB_EXIT
        jump = done & (in3 < n_items) & (nxt < n_items) & (qi_tab[nxt] == tile3)
        ptr = jnp.where(jump, ts_tab[tile3 + 1], nxt)
        return (ptr, in1, in2, in3, carries[0], carries[1], accs[0], accs[1])

    def body(st):
        return advance(1, advance(0, st))

    def pending(st):
        ptr, in2, in3, in4 = st[:4]
        return (ptr < n_items) | (in2 < n_items) | (in3 < n_items) | (in4 < n_items)

    zeros = jnp.zeros((tq, LANES), F32)
    filler = jnp.int32(n_items)
    lax.while_loop(pending, body, (jnp.int32(0), filler, filler, filler, zeros, zeros, zeros, zeros))


def sb_attention(proj, q_gain, k_gain, *, batch, seq, tq, tk):
    tq = min(tq, seq)
    tk = min(tk, tq)
    assert seq % tq == 0 and tq % tk == 0
    n_pairs = SB_HEADS // 2
    n_items, qi, kj, tile_start = _sb_items(seq // tq, tq // tk)
    g2 = lambda g: jnp.concatenate([g, g]).reshape(1, LANES).astype(F32)
    seq_blk = lambda s0: pl.BlockSpec((1, seq, LANES), lambda b, p, qt, kt, ts: (s0 + p, b, 0))
    stage = lambda dt: pltpu.VMEM((2, 2, tq, tk), dt)
    return pl.pallas_call(
        functools.partial(_sb_kernel, tq=tq, tk=tk, seq=seq, n_items=n_items),
        grid_spec=pltpu.PrefetchScalarGridSpec(
            num_scalar_prefetch=3,
            grid=(batch, n_pairs),
            in_specs=[
                seq_blk(SLAB_SB_Q), seq_blk(SLAB_SB_K), seq_blk(SLAB_SB_V),
                pl.BlockSpec((1, LANES), lambda b, p, qt, kt, ts: (0, 0)),
                pl.BlockSpec((1, LANES), lambda b, p, qt, kt, ts: (0, 0)),
            ],
            out_specs=pl.BlockSpec((1, seq, LANES), lambda b, p, qt, kt, ts: (p, b, 0)),
            scratch_shapes=[
                pltpu.VMEM((2, seq, LANES), BF16),
                pltpu.VMEM((seq, LANES), BF16),
                stage(F32), stage(BF16), stage(BF16), stage(BF16),
                pltpu.VMEM((2, 2, tq, LANES), F32),
            ],
        ),
        out_shape=jax.ShapeDtypeStruct((n_pairs, batch * seq, LANES), BF16),
        compiler_params=_params("parallel", "parallel"),
        name="sb_attention",
    )(jnp.asarray(qi, jnp.int32), jnp.asarray(kj, jnp.int32), jnp.asarray(tile_start, jnp.int32), proj, proj, proj, g2(q_gain), g2(k_gain))


GLA_ROWS = 256
GLA_BLOCKS_PER_STEP = 2
GLA_KW = GLA_HEADS * GLA_DK


def _gla_kernel(q_ref, k_ref, v_ref, a_ref, r_ref, aup_ref, ab_ref, og_ref, o_ref, state_ref):
    @pl.when(pl.program_id(1) == 0)
    def _():
        state_ref[...] = jnp.zeros_like(state_ref)

    state = state_ref[...]
    for blk in range(o_ref.shape[1] // GLA_ROWS):
        state = _gla_block(blk * GLA_ROWS, state, q_ref, k_ref, v_ref, a_ref, r_ref,
                           aup_ref, ab_ref, og_ref, o_ref)
    state_ref[...] = state


def _gla_block(b0, state, q_ref, k_ref, v_ref, a_ref, r_ref, aup_ref, ab_ref, og_ref, o_ref):
    rows = GLA_ROWS
    n_chunks = rows // GLA_CHUNK
    blk = pl.ds(b0, rows)

    pre = jnp.dot(a_ref[0, blk, :], aup_ref[...], preferred_element_type=F32) + ab_ref[...]
    log_a = -_softplus(-pre) * (1.0 / GLA_GATE_TAU)
    la_hi, la_lo = _split_bf16(log_a)

    ri = lax.broadcasted_iota(jnp.int32, (rows, rows), 0)
    ci = lax.broadcasted_iota(jnp.int32, (rows, rows), 1)
    same_chunk = (ri // GLA_CHUNK) == (ci // GLA_CHUNK)
    tri = same_chunk & (ri >= ci)
    tri_b = jnp.where(tri, 1.0, 0.0).astype(BF16)
    ones_b = jnp.where(same_chunk, 1.0, 0.0).astype(BF16)
    cum = (jnp.dot(tri_b, la_hi, preferred_element_type=F32)
           + jnp.dot(tri_b, la_lo, preferred_element_type=F32))
    cum_last = (jnp.dot(ones_b, la_hi, preferred_element_type=F32)
                + jnp.dot(ones_b, la_lo, preferred_element_type=F32))

    q_all = jnp.concatenate([q_ref[0, blk, :], q_ref[1, blk, :]], axis=-1).astype(F32)
    k_all = jnp.concatenate([k_ref[0, blk, :], k_ref[1, blk, :]], axis=-1).astype(F32)
    q_dec = q_all * (GLA_DK ** -0.5) * jnp.exp(cum)
    k_inv = (k_all * jnp.exp(-cum)).astype(BF16)
    k_end = (k_all * jnp.exp(cum_last - cum)).astype(BF16)
    v_heads = [v_ref[h, blk, :] for h in range(GLA_HEADS)]
    v_all = jnp.concatenate(v_heads, axis=-1)

    lane = lax.broadcasted_iota(jnp.int32, (rows, GLA_KW), 1)
    q_heads = [jnp.where((lane // GLA_DK) == h, q_dec, 0.0).astype(BF16) for h in range(GLA_HEADS)]

    o_heads = []
    for h in range(GLA_HEADS):
        sc = lax.dot_general(q_heads[h], k_inv, (((1,), (1,)), ((), ())), preferred_element_type=F32)
        sc = jnp.where(tri, sc, 0.0).astype(BF16)
        o_heads.append(jnp.dot(sc, v_heads[h], preferred_element_type=F32))

    ones_cols = jnp.ones((GLA_CHUNK, LANES), BF16)
    inter =[[] for _ in range(GLA_HEADS)]
    for c in range(n_chunks):
        r0, r1 = c * GLA_CHUNK, (c + 1) * GLA_CHUNK
        state_b = state.astype(BF16)
        for h in range(GLA_HEADS):
            inter[h].append(jnp.dot(q_heads[h][r0:r1], state_b, preferred_element_type=F32))
        kv = lax.dot_general(k_end[r0:r1], v_all[r0:r1], (((0,), (0,)), ((), ())),
                             preferred_element_type=F32)
        kv_sel = jnp.concatenate(
            [kv[h * GLA_DK:(h + 1) * GLA_DK, h * GLA_DV:(h + 1) * GLA_DV] for h in range(GLA_HEADS)], axis=0)
        tot = (lax.dot_general(la_hi[r0:r1], ones_cols, (((0,), (0,)), ((), ())), preferred_element_type=F32)
               + lax.dot_general(la_lo[r0:r1], ones_cols, (((0,), (0,)), ((), ())), preferred_element_type=F32))
        state = jnp.exp(tot) * state + kv_sel

    for h in range(GLA_HEADS):
        o = o_heads[h] + jnp.concatenate(inter[h], axis=0)
        o = _rms_rows(o) * og_ref[...]
        r = r_ref[h, blk, :].astype(F32)
        o_ref[h, blk, :] = (o * (r * jax.nn.sigmoid(r))).astype(o_ref.dtype)
    return state


def gla(proj, a_up, a_b, o_gain, *, batch, seq):
    rows = GLA_ROWS * GLA_BLOCKS_PER_STEP
    if seq % rows:
        rows = GLA_ROWS
    assert seq % rows == 0
    nb = seq // rows
    a_up_p = jnp.zeros((LANES, GLA_KW), BF16).at[:GLA_GATE_RANK].set(a_up.astype(BF16))
    row_blk = lambda n, s0: pl.BlockSpec((n, rows, LANES), lambda b, i: (s0 // n, b * nb + i, 0))
    return pl.pallas_call(
        _gla_kernel,
        grid=(batch, nb),
        in_specs=[
            row_blk(2, SLAB_GLA_Q), row_blk(2, SLAB_GLA_K), row_blk(4, SLAB_GLA_V),
            row_blk(1, SLAB_GLA_A), row_blk(4, SLAB_GLA_R),
            pl.BlockSpec((LANES, GLA_KW), lambda b, i: (0, 0)),
            pl.BlockSpec((1, GLA_KW), lambda b, i: (0, 0)),
            pl.BlockSpec((1, GLA_DV), lambda b, i: (0, 0)),
        ],
        out_specs=pl.BlockSpec((GLA_HEADS, rows, LANES), lambda b, i: (0, b * nb + i, 0)),
        out_shape=jax.ShapeDtypeStruct((GLA_HEADS, batch * seq, LANES), BF16),
        scratch_shapes=[pltpu.VMEM((GLA_KW, GLA_DV), F32)],
        compiler_params=_params("parallel", "arbitrary"),
        name="gla",
    )(proj, proj, proj, proj, proj, a_up_p,
      a_b.reshape(1, GLA_KW).astype(F32), o_gain.reshape(1, GLA_DV).astype(F32))


def _merge_kernel(x_ref, oa_ref, ob_ref, cq_ref, gate_ref, mkv_ref, cqg_ref, ckg_ref,
                  wsb_ref, wgla_ref, wca_ref, wout_ref, o_ref):
    ca_w = CA_HEADS * CA_HEAD_DIM
    oc = []
    for h in range(CA_HEADS):
        q = _rms_rows(cq_ref[h].astype(F32)) * (cqg_ref[...] * CA_HEAD_DIM ** -0.5)
        kk = _rms_rows(mkv_ref[:, h * CA_HEAD_DIM:(h + 1) * CA_HEAD_DIM].astype(F32)) * ckg_ref[...]
        s = lax.dot_general(q.astype(BF16), kk.astype(BF16), (((1,), (1,)), ((), ())),
                            preferred_element_type=F32)
        e = jnp.exp(s - jnp.max(s, axis=-1, keepdims=True))
        p = e * (1.0 / jnp.sum(e, axis=-1, keepdims=True))
        vv = mkv_ref[:, ca_w + h * CA_HEAD_DIM: ca_w + (h + 1) * CA_HEAD_DIM]
        oc.append(jnp.dot(p.astype(BF16), vv, preferred_element_type=F32).astype(BF16))
    o_c = jnp.concatenate(oc, axis=-1)
    o_a = jnp.concatenate([oa_ref[s] for s in range(4)], axis=-1)
    o_b = jnp.concatenate([ob_ref[s] for s in range(4)], axis=-1)

    def gate(b):
        g = jnp.concatenate([gate_ref[b * 8 + s] for s in range(8)], axis=-1).astype(F32)
        return jax.nn.sigmoid(g)

    merged = gate(0) * jnp.dot(o_a, wsb_ref[...], preferred_element_type=F32)
    merged += gate(1) * jnp.dot(o_b, wgla_ref[...], preferred_element_type=F32)
    merged += gate(2) * jnp.dot(o_c, wca_ref[...], preferred_element_type=F32)
    o_ref[...] = x_ref[...] + jnp.dot(merged.astype(BF16), wout_ref[...], preferred_element_type=F32)


def merge_branches(x, o_a, o_b, proj, mkv, cq_gain, ck_gain, w_sb, w_gla, w_ca, w_out, *, batch, seq, tm):
    n, d = x.shape
    tm = min(tm, seq)
    assert seq % tm == 0
    nt = seq // tm
    mem_len = mkv.shape[0] // batch
    full = lambda a: pl.BlockSpec(a.shape, lambda i: (0,) * a.ndim)
    cqg = cq_gain.reshape(1, CA_HEAD_DIM).astype(F32)
    ckg = ck_gain.reshape(1, CA_HEAD_DIM).astype(F32)
    return pl.pallas_call(
        _merge_kernel,
        grid=(n // tm,),
        in_specs=[
            pl.BlockSpec((tm, d), lambda i: (i, 0)),
            pl.BlockSpec((4, tm, LANES), lambda i: (0, i, 0)),
            pl.BlockSpec((4, tm, LANES), lambda i: (0, i, 0)),
            pl.BlockSpec((4, tm, LANES), lambda i: (SLAB_CA_Q // 4, i, 0)),
            pl.BlockSpec((24, tm, LANES), lambda i: (SLAB_GATES // 24, i, 0)),
            pl.BlockSpec((mem_len, mkv.shape[1]), lambda i: (i // nt, 0)),
            full(cqg), full(ckg), full(w_sb), full(w_gla), full(w_ca), full(w_out),
        ],
        out_specs=pl.BlockSpec((tm, d), lambda i: (i, 0)),
        out_shape=jax.ShapeDtypeStruct((n, d), F32),
        compiler_params=_params("parallel"),
        name="merge_branches",
    )(x, o_a, o_b, proj, proj, mkv,
      cqg, ckg, w_sb, w_gla, w_ca, w_out)


def _ffn_kernel(x_ref, g_ref, wg_ref, wu_ref, wd_ref, o_ref, hn_ref, acc_ref):
    f = pl.program_id(1)

    @pl.when(f == 0)
    def _():
        hn_ref[...] = (_rms_rows(x_ref[...]) * g_ref[...]).astype(BF16)
        acc_ref[...] = jnp.zeros_like(acc_ref)

    h = hn_ref[...]
    gt = jnp.dot(h, wg_ref[...], preferred_element_type=F32)
    up = jnp.dot(h, wu_ref[...], preferred_element_type=F32)
    act = (gt * jax.nn.sigmoid(gt) * up).astype(BF16)
    acc_ref[...] += jnp.dot(act, wd_ref[...], preferred_element_type=F32)

    @pl.when(f == pl.num_programs(1) - 1)
    def _():
        o_ref[...] = x_ref[...] + acc_ref[...]


def ffn_dense(x, g, w_gate_up, w_down, *, tm, tf):
    n, d = x.shape
    d_ff = w_down.shape[0]
    tm = min(tm, n)
    assert n % tm == 0 and d_ff % tf == 0 and tf % LANES == 0
    nf = d_ff // tf
    return pl.pallas_call(
        _ffn_kernel,
        grid=(n // tm, nf),
        in_specs=[
            pl.BlockSpec((tm, d), lambda i, f: (i, 0)),
            pl.BlockSpec((1, d), lambda i, f: (0, 0)),
            pl.BlockSpec((d, tf), lambda i, f: (0, f)),
            pl.BlockSpec((d, tf), lambda i, f: (0, nf + f)),
            pl.BlockSpec((tf, d), lambda i, f: (f, 0)),
        ],
        out_specs=pl.BlockSpec((tm, d), lambda i, f: (i, 0)),
        out_shape=jax.ShapeDtypeStruct((n, d), F32),
        scratch_shapes=[pltpu.VMEM((tm, d), BF16), pltpu.VMEM((tm, d), F32)],
        compiler_params=_params("parallel", "arbitrary"),
        name="ffn_dense",
    )(x, g.reshape(1, d), w_gate_up, w_gate_up, w_down)


def _router_kernel(x_ref, g_ref, wr_ref, c_ref):
    hn = _rms_rows(x_ref[...]) * g_ref[...]
    logits = jnp.dot(hn, wr_ref[...], preferred_element_type=F32, precision=lax.Precision.HIGHEST)
    lane = lax.broadcasted_iota(jnp.int32, logits.shape, 1).astype(F32)
    neg = jnp.float32(-jnp.inf)
    logits = jnp.where(lane < N_EXPERTS, logits, neg)
    m1 = jnp.max(logits, axis=-1, keepdims=True)
    i1 = jnp.min(jnp.where(logits == m1, lane, float(LANES)), axis=-1, keepdims=True)
    rest = jnp.where(lane == i1, neg, logits)
    m2 = jnp.max(rest, axis=-1, keepdims=True)
    i2 = jnp.min(jnp.where(rest == m2, lane, float(LANES)), axis=-1, keepdims=True)
    e2 = jnp.exp(m2 - m1)
    inv = 1.0 / (1.0 + e2)
    c_ref[...] = (jnp.where(lane == 0.0, inv, 0.0) + jnp.where(lane == 1.0, e2 * inv, 0.0)
                  + jnp.where(lane == 2.0, i1, 0.0) + jnp.where(lane == 3.0, i2, 0.0))


def router(x, g, w_router, *, tm):
    n, d = x.shape
    tm = min(tm, n)
    wr = jnp.zeros((d, LANES), F32).at[:, :N_EXPERTS].set(w_router)
    return pl.pallas_call(
        _router_kernel,
        grid=(n // tm,),
        in_specs=[
            pl.BlockSpec((tm, d), lambda i: (i, 0)),
            pl.BlockSpec((1, d), lambda i: (0, 0)),
            pl.BlockSpec((d, LANES), lambda i: (0, 0)),
        ],
        out_specs=pl.BlockSpec((tm, LANES), lambda i: (i, 0)),
        out_shape=jax.ShapeDtypeStruct((n, LANES), F32),
        compiler_params=_params("parallel"),
        name="router",
    )(x, g.reshape(1, d), wr)


def _row_copy(src_hbm, row, dst_vmem, r, sem):
    return pltpu.make_async_copy(src_hbm.at[pl.ds(row, 1)], dst_vmem.at[pl.ds(r, 1)], sem)


def _expert_ffn_kernel(te_ref, nu_ref, cur_ref, nxt_ref, x_hbm, g_ref, wg_ref, wu_ref, wd_ref, o_ref,
                       xbuf_ref, sem, hn_ref, acc_ref, *, nf):
    i = pl.program_id(0)
    f = pl.program_id(1)
    rows = o_ref.shape[0]
    used = i < nu_ref[0]
    slot = i % 2

    def normalise(s):
        hn_ref[s] = (_rms_rows(xbuf_ref[...]) * g_ref[...]).astype(BF16)

    @pl.when((i == 0) & (f == 0))
    def _():
        def start(r, carry):
            _row_copy(x_hbm, cur_ref[0, 0, r], xbuf_ref, r, sem).start()
            return carry
        lax.fori_loop(0, rows, start, 0, unroll=GATHER_UNROLL)

        def wait(r, carry):
            _row_copy(x_hbm, 0, xbuf_ref, r, sem).wait()
            return carry
        lax.fori_loop(0, rows, wait, 0, unroll=GATHER_UNROLL)
        normalise(0)

    def ffn_step(first, last):
        if first:
            for r in range(rows):
                _row_copy(x_hbm, nxt_ref[0, 0, r], xbuf_ref, r, sem).start()
        h = hn_ref[slot]
        gt = jnp.dot(h, wg_ref[0], preferred_element_type=F32)
        up = jnp.dot(h, wu_ref[0], preferred_element_type=F32)
        act = (gt * jax.nn.sigmoid(gt) * up).astype(BF16)
        part = jnp.dot(act, wd_ref[0], preferred_element_type=F32)
        total = part if first else acc_ref[...] + part
        if last:
            o_ref[...] = total
            for r in range(rows):
                _row_copy(x_hbm, 0, xbuf_ref, r, sem).wait()
            normalise(1 - slot)
        else:
            acc_ref[...] = total

    if nf == 1:
        pl.when(used)(lambda: ffn_step(True, True))
    else:
        pl.when(used & (f == 0))(lambda: ffn_step(True, False))
        pl.when(used & (f == nf - 1))(lambda: ffn_step(False, True))
        if nf > 2:
            pl.when(used & (f > 0) & (f < nf - 1))(lambda: ffn_step(False, False))

    @pl.when(jnp.logical_not(used) & (f == nf - 1))
    def _():
        o_ref[...] = jnp.zeros_like(o_ref)


def expert_ffn(x, src, g, tile_expert, n_used, w_gate_up, w_down, *, tm, tf):
    rows = src.shape[0]
    d = x.shape[1]
    d_ff = w_down.shape[1]
    assert rows % tm == 0 and d_ff % tf == 0 and tf % LANES == 0
    nf = d_ff // tf
    n_tiles = rows // tm
    src = src.reshape(n_tiles, 1, tm)
    return pl.pallas_call(
        functools.partial(_expert_ffn_kernel, nf=nf),
        grid_spec=pltpu.PrefetchScalarGridSpec(
            num_scalar_prefetch=2,
            grid=(n_tiles, nf),
            in_specs=[
                pl.BlockSpec((1, 1, tm), lambda i, f, te, nu: (i, 0, 0), memory_space=pltpu.SMEM),
                pl.BlockSpec((1, 1, tm), lambda i, f, te, nu: (jnp.minimum(i + 1, n_tiles - 1), 0, 0),
                             memory_space=pltpu.SMEM),
                pl.BlockSpec(memory_space=pl.ANY),
                pl.BlockSpec((1, d), lambda i, f, te, nu: (0, 0)),
                pl.BlockSpec((1, d, tf), lambda i, f, te, nu: (te[i], 0, f)),
                pl.BlockSpec((1, d, tf), lambda i, f, te, nu: (te[i], 0, nf + f)),
                pl.BlockSpec((1, tf, d), lambda i, f, te, nu: (te[i], f, 0)),
            ],
            out_specs=pl.BlockSpec((tm, d), lambda i, f, te, nu: (i, 0)),
            scratch_shapes=[pltpu.VMEM((tm, d), F32), pltpu.SemaphoreType.DMA,
                            pltpu.VMEM((2, tm, d), BF16), pltpu.VMEM((tm, d), F32)],
        ),
        out_shape=jax.ShapeDtypeStruct((rows, d), F32),
        compiler_params=_params("arbitrary", "arbitrary"),
        name="expert_ffn",
    )(tile_expert, n_used, src, src, x, g.reshape(1, d), w_gate_up, w_gate_up, w_down)


def _combine_kernel(p1_ref, p2_ref, n1_ref, n2_ref, x_ref, rw_ref, ys_hbm, o_ref, a_ref, b_ref, sem):
    i = pl.program_id(0)
    rows = o_ref.shape[0]
    slot = i % 2

    def issue(q1_ref, q2_ref, s):
        def one(r, carry):
            _row_copy(ys_hbm, q1_ref[0, 0, r], a_ref.at[s], r, sem.at[0, s]).start()
            _row_copy(ys_hbm, q2_ref[0, 0, r], b_ref.at[s], r, sem.at[1, s]).start()
            return carry
        lax.fori_loop(0, rows, one, 0, unroll=GATHER_UNROLL)

    @pl.when(i == 0)
    def _():
        issue(p1_ref, p2_ref, 0)

    @pl.when(i + 1 < pl.num_programs(0))
    def _():
        issue(n1_ref, n2_ref, 1 - slot)

    def drain(r, carry):
        _row_copy(ys_hbm, 0, a_ref.at[slot], r, sem.at[0, slot]).wait()
        _row_copy(ys_hbm, 0, b_ref.at[slot], r, sem.at[1, slot]).wait()
        return carry
    lax.fori_loop(0, rows, drain, 0, unroll=GATHER_UNROLL)

    rw = rw_ref[...]
    o_ref[...] = x_ref[...] + (rw[:, 0:1] * a_ref[slot] + rw[:, 1:2] * b_ref[slot])


def moe_combine(x, route, ys, pos1, pos2, *, tc):
    n, d = x.shape
    assert n % tc == 0
    nt = n // tc
    idx_spec = pl.BlockSpec((1, 1, tc), lambda i: (i, 0, 0), memory_space=pltpu.SMEM)
    nxt_spec = pl.BlockSpec((1, 1, tc), lambda i: (jnp.minimum(i + 1, nt - 1), 0, 0), memory_space=pltpu.SMEM)
    pos1 = pos1.reshape(nt, 1, tc)
    pos2 = pos2.reshape(nt, 1, tc)
    return pl.pallas_call(
        _combine_kernel,
        grid=(nt,),
        in_specs=[
            idx_spec, idx_spec, nxt_spec, nxt_spec,
            pl.BlockSpec((tc, d), lambda i: (i, 0)),
            pl.BlockSpec((tc, LANES), lambda i: (i, 0)),
            pl.BlockSpec(memory_space=pl.ANY),
        ],
        out_specs=pl.BlockSpec((tc, d), lambda i: (i, 0)),
        out_shape=jax.ShapeDtypeStruct((n, d), F32),
        scratch_shapes=[pltpu.VMEM((2, tc, d), F32), pltpu.VMEM((2, tc, d), F32),
                        pltpu.SemaphoreType.DMA((2, 2))],
        compiler_params=_params("arbitrary"),
        name="moe_combine",
    )(pos1, pos2, pos1, pos2, x, route, ys)


def _dispatch_plan(route, *, tm):
    n = route.shape[0]
    ids = route[:, 2:4].astype(jnp.int32)
    e_flat = ids.T.reshape(-1)
    onehot = (e_flat[:, None] == jnp.arange(N_EXPERTS)[None, :]).astype(jnp.int32)
    running = jnp.cumsum(onehot, axis=0)
    rank = jnp.sum(running * onehot, axis=1) - 1
    counts = running[-1]
    padded = ((counts + tm - 1) // tm) * tm
    ends = jnp.cumsum(padded)
    starts = ends - padded
    pos = starts[e_flat] + rank
    rows = TOP_K * n + N_EXPERTS * tm
    tok = jnp.tile(jnp.arange(n, dtype=jnp.int32), TOP_K)
    src = jnp.zeros((rows,), jnp.int32).at[pos].set(tok)
    tile_start = jnp.arange(rows // tm, dtype=jnp.int32) * tm
    tile_expert = jnp.minimum(jnp.sum(tile_start[:, None] >= ends[None, :], axis=1), N_EXPERTS - 1)
    n_used = (ends[-1] // tm).reshape(1).astype(jnp.int32)
    return src, tile_expert.astype(jnp.int32), n_used, pos[:n].astype(jnp.int32), pos[n:].astype(jnp.int32)


def ffn_moe(x, g, route, w_gate_up, w_down, *, tm, tf, tc):
    src, tile_expert, n_used, pos1, pos2 = _dispatch_plan(route, tm=tm)
    ys = expert_ffn(x, src, g, tile_expert, n_used, w_gate_up, w_down, tm=tm, tf=tf)
    return moe_combine(x, route, ys, pos1, pos2, tc=min(tc, x.shape[0]))


def _slab_weights(w_in):
    d = w_in.shape[0]
    sbw = SB_HEADS * SB_HEAD_DIM
    glv = GLA_HEADS * GLA_DV
    caw = CA_HEADS * CA_HEAD_DIM
    sizes = (sbw, sbw, sbw, GLA_KW, GLA_KW, glv, GLA_GATE_RANK, glv, caw, N_BRANCH * D_MODEL)
    parts, off = [], 0
    for s in sizes:
        parts.append(w_in[:, off:off + s])
        off += s
    sb_q, sb_k, sb_v, g_q, g_k, g_v, g_a, g_r, ca_q, gates = parts
    pad = jnp.zeros((d, 2 * LANES - GLA_GATE_RANK), w_in.dtype)
    cols = [sb_q, sb_k, sb_v, g_v, g_r, ca_q, gates, g_q, g_k, g_a, pad]
    w = jnp.concatenate([c.astype(BF16) for c in cols], axis=1)
    assert w.shape[1] == N_SLABS * LANES
    return w


def kernel(x, mem, mix_norm_g, w_in, sb_q_g, sb_k_g, gla_a_up, gla_a_b, gla_o_g, ca_q_g, ca_k_g, mem_norm_g, w_mem_kv, w_br_sb, w_br_gla, w_br_ca, w_out, ffn_norm_g, ff_w_gate_up, ff_w_down, moe_w_router, moe_w_gate_up, moe_w_down):
    batch, seq, d = x.shape
    depth = w_in.shape[0]
    n = batch * seq
    xf = x.reshape(n, d)
    memf = mem.reshape(-1, d)
    bf = lambda a: a.astype(BF16)

    t = TILES
    for layer in range(depth):
        proj = norm_matmul(xf, mix_norm_g[layer], _slab_weights(w_in[layer]), **t["in_proj"], slabs=True)
        mkv = norm_matmul(memf, mem_norm_g[layer], bf(w_mem_kv[layer]), **t["mem_kv"], slabs=False)
        o_a = sb_attention(proj, sb_q_g[layer], sb_k_g[layer], batch=batch, seq=seq, **t["sb"])
        o_b = gla(proj, gla_a_up[layer], gla_a_b[layer], gla_o_g[layer], batch=batch, seq=seq)
        xf = merge_branches(xf, o_a, o_b, proj, mkv, ca_q_g[layer], ca_k_g[layer],
                            bf(w_br_sb[layer]), bf(w_br_gla[layer]), bf(w_br_ca[layer]), bf(w_out[layer]),
                            batch=batch, seq=seq, **t["merge"])
        j = layer // 2
        if layer % 2 == 0:
            xf = ffn_dense(xf, ffn_norm_g[layer], bf(ff_w_gate_up[j]), bf(ff_w_down[j]), **t["ffn"])
        else:
            route = router(xf, ffn_norm_g[layer], moe_w_router[j], **t["router"])
            xf = ffn_moe(xf, ffn_norm_g[layer], route, bf(moe_w_gate_up[j]), bf(moe_w_down[j]), **t["moe"])
    return xf.reshape(batch, seq, d)
```

```python
import functools

import jax
import jax.numpy as jnp
from jax import lax
from jax.experimental import pallas as pl
from jax.experimental.pallas import tpu as pltpu

F32 = jnp.float32
BF16 = jnp.bfloat16

LANES = 128
EPS = 1e-6
LOG2E = 1.4426950408889634
MASKED = 1e30
GATHER_UNROLL = 8
SB_STAGES = 4
SB_EXIT = -110.0
VMEM_LIMIT_BYTES = 56 * 1024 * 1024

D_MODEL = 1024
SB_HEADS, SB_HEAD_DIM = 8, 64
GLA_HEADS, GLA_DK, GLA_DV = 4, 64, 128
GLA_GATE_RANK, GLA_GATE_TAU, GLA_CHUNK = 16, 16.0, 64
CA_HEADS, CA_HEAD_DIM = 4, 128
N_BRANCH = 3
N_EXPERTS, TOP_K = 8, 2

SLAB_SB_Q, SLAB_SB_K, SLAB_SB_V = 0, 4, 8
SLAB_GLA_V, SLAB_GLA_R = 12, 16
SLAB_CA_Q = 20
SLAB_GATES = 24
SLAB_GLA_Q, SLAB_GLA_K, SLAB_GLA_A = 48, 50, 52
N_SLABS = 54


TILES = {
    "in_proj": dict(tm=1024, tn=2304),
    "mem_kv": dict(tm=512, tn=1024),
    "sb": dict(tq=256, tk=128),
    "merge": dict(tm=1024),
    "ffn": dict(tm=1024, tf=1408),
    "router": dict(tm=512),
    "moe": dict(tm=512, tf=1792, tc=256),
}


def _params(*semantics):
    return pltpu.CompilerParams(dimension_semantics=semantics, vmem_limit_bytes=VMEM_LIMIT_BYTES)


def _rms_rows(x, eps=EPS):
    return x * lax.rsqrt(jnp.mean(x * x, axis=-1, keepdims=True) + eps)


def _rms_half_lanes(x, eps=EPS):
    lane = lax.broadcasted_iota(jnp.int32, x.shape, 1)
    lo = lane < SB_HEAD_DIM
    sq = x * x
    s_lo = jnp.sum(jnp.where(lo, sq, 0.0), axis=-1, keepdims=True)
    s_hi = jnp.sum(jnp.where(lo, 0.0, sq), axis=-1, keepdims=True)
    ms = jnp.where(lo, s_lo, s_hi) * (1.0 / SB_HEAD_DIM)
    return x * lax.rsqrt(ms + eps)


def _split_bf16(x):
    hi = x.astype(BF16)
    lo = (x - hi.astype(F32)).astype(BF16)
    return hi, lo


def _norm_matmul_kernel(x_ref, g_ref, w_ref, o_ref, xn_ref, *, n_slab):
    @pl.when(pl.program_id(1) == 0)
    def _():
        xn_ref[...] = (_rms_rows(x_ref[...]) * g_ref[...]).astype(BF16)

    acc = jnp.dot(xn_ref[...], w_ref[...], preferred_element_type=F32)
    if n_slab is None:
        o_ref[...] = acc.astype(o_ref.dtype)
    else:
        for s in range(n_slab):
            o_ref[s] = acc[:, s * LANES:(s + 1) * LANES].astype(o_ref.dtype)


def norm_matmul(x, g, w, *, tm, tn, slabs):
    n, k = x.shape
    m = w.shape[1]
    tm = min(tm, n)
    tn = min(tn, m)
    assert n % tm == 0 and m % tn == 0 and tn % LANES == 0
    if slabs:
        n_slab = tn // LANES
        out_shape = jax.ShapeDtypeStruct((m // LANES, n, LANES), BF16)
        out_spec = pl.BlockSpec((n_slab, tm, LANES), lambda i, j: (j, i, 0))
    else:
        n_slab = None
        out_shape = jax.ShapeDtypeStruct((n, m), BF16)
        out_spec = pl.BlockSpec((tm, tn), lambda i, j: (i, j))
    return pl.pallas_call(
        functools.partial(_norm_matmul_kernel, n_slab=n_slab),
        grid=(n // tm, m // tn),
        in_specs=[
            pl.BlockSpec((tm, k), lambda i, j: (i, 0)),
            pl.BlockSpec((1, k), lambda i, j: (0, 0)),
            pl.BlockSpec((k, tn), lambda i, j: (0, j)),
        ],
        out_specs=out_spec,
        out_shape=out_shape,
        scratch_shapes=[pltpu.VMEM((tm, k), BF16)],
        compiler_params=_params("parallel", "arbitrary"),
        name="norm_matmul",
    )(x, g.reshape(1, k), w)


def _softplus(z):
    return jnp.maximum(z, 0.0) + jnp.log(1.0 + jnp.exp(-jnp.abs(z)))


def _sb_items(nq, n_diag):
    qi, kj = [], []
    for i in range(nq):
        for j in reversed(range((i + 1) * n_diag)):
            qi.append(i)
            kj.append(j)
    n_items = len(qi)
    tile_start = [i * (i + 1) // 2 * n_diag for i in range(nq + 1)]
    assert tile_start[nq] == n_items
    qi += [0] * n_diag * SB_STAGES
    kj += list(reversed(range(n_diag))) * SB_STAGES
    return n_items, qi, kj, tile_start


def _sb_kernel(qi_tab, kj_tab, ts_tab, q_ref, k_ref, v_ref, qg_ref, kg_ref, o_ref,
               qh_ref, kn_ref, z_ref, sp_ref, bt_ref, w_ref, rs_ref, *, tq, tk, seq, n_items):
    n_diag = tq // tk
    rows = min(512, seq)
    lane = lax.broadcasted_iota(jnp.int32, (rows, LANES), 1)
    scale = SB_HEAD_DIM ** -0.5

    def norm_chunk(c, carry):
        r0 = pl.multiple_of(c * rows, rows)
        kk = k_ref[0, pl.ds(r0, rows), :].astype(F32)
        kn_ref[pl.ds(r0, rows), :] = (_rms_half_lanes(kk) * kg_ref[...]).astype(BF16)
        qn = _rms_half_lanes(q_ref[0, pl.ds(r0, rows), :].astype(F32)) * (qg_ref[...] * scale)
        qh_ref[0, pl.ds(r0, rows), :] = jnp.where(lane < SB_HEAD_DIM, qn, 0.0).astype(BF16)
        qh_ref[1, pl.ds(r0, rows), :] = jnp.where(lane >= SB_HEAD_DIM, qn, 0.0).astype(BF16)
        return carry
    lax.fori_loop(0, seq // rows, norm_chunk, 0)

    z_ref[...] = jnp.zeros_like(z_ref)
    sp_ref[...] = jnp.zeros_like(sp_ref)
    bt_ref[...] = jnp.zeros_like(bt_ref)
    w_ref[...] = jnp.zeros_like(w_ref)
    rs_ref[...] = jnp.zeros_like(rs_ref)
    o_ref[...] = jnp.zeros_like(o_ref)

    rj = lax.broadcasted_iota(jnp.int32, (tk, tk), 0)
    cs = lax.broadcasted_iota(jnp.int32, (tk, tk), 1)
    u_strict = jnp.where(rj > cs, -1.0, 0.0).astype(BF16)
    rel = (lax.broadcasted_iota(jnp.int32, (tq, tk), 1) - lax.broadcasted_iota(jnp.int32, (tq, tk), 0))
    out_lane = lax.broadcasted_iota(jnp.int32, (tq, LANES), 1) < SB_HEAD_DIM

    def item(m):
        return qi_tab[m], kj_tab[m]

    def step(slot, ids, carries, accs):
        prev = 1 - slot
        qi1, kj1 = item(ids[0])
        kb = kn_ref[pl.ds(pl.multiple_of(kj1 * tk, tk), tk), :]
        for h in range(2):
            qh = qh_ref[h, pl.ds(pl.multiple_of(qi1 * tq, tq), tq), :]
            z_ref[slot, h] = lax.dot_general(qh, kb, (((1,), (1,)), ((), ())), preferred_element_type=F32)
        qi2, kj2 = item(ids[1])
        keep = rel < (qi2 * tq - kj2 * tk)
        for h in range(2):
            z = jnp.where(keep, z_ref[prev, h], -MASKED)
            e = jnp.exp2(jnp.abs(z) * (-LOG2E))
            r = pl.reciprocal(1.0 + e)
            sp = jnp.maximum(z, 0.0) - jnp.log(r)
            beta = jnp.where(z >= 0.0, r, e * r)
            sp_ref[slot, h] = sp.astype(BF16)
            bt_ref[slot, h] = beta.astype(BF16)
            rs_ref[slot, h] = jnp.broadcast_to(jnp.sum(sp, axis=1, keepdims=True), (tq, LANES))
        qi3, kj3 = item(ids[2])
        first3 = kj3 == (qi3 + 1) * n_diag - 1
        new_carries = []
        for h in range(2):
            t = jnp.dot(sp_ref[prev, h], u_strict, preferred_element_type=F32)
            c_in = jnp.where(first3, 0.0, carries[h])
            c_wide = jnp.concatenate([c_in] * (tk // LANES), axis=1)
            w = jnp.exp2((t + c_wide) * LOG2E) * bt_ref[prev, h].astype(F32)
            w_ref[slot, h] = w.astype(BF16)
            new_carries.append(c_in - rs_ref[prev, h])
        qi4, kj4 = item(ids[3])
        first4 = kj4 == (qi4 + 1) * n_diag - 1
        vb = v_ref[0, pl.ds(pl.multiple_of(kj4 * tk, tk), tk), :]
        new_accs = []
        for h in range(2):
            pv = jnp.dot(w_ref[prev, h], vb, preferred_element_type=F32)
            new_accs.append(jnp.where(first4, 0.0, accs[h]) + pv)
        out_rows = pl.ds(pl.multiple_of(qi4 * tq, tq), tq)
        result = jnp.where(out_lane, new_accs[0], new_accs[1]).astype(o_ref.dtype)
        o_ref[0, out_rows, :] = jnp.where(ids[3] < n_items, result, o_ref[0, out_rows, :])
        return new_carries, new_accs

    def advance(slot, st):
        ptr, in2, in3, in4 = st[:4]
        carries, accs = [st[4], st[5]], [st[6], st[7]]
        in1 = jnp.minimum(ptr, n_items)
        carries, accs = step(slot, (in1, in2, in3, in4), carries, accs)
        nxt = jnp.minimum(ptr + 1, n_items)
        tile3 = qi_tab[in3]
        done = jnp.max(jnp.maximum(carries[0], carries[1])) < SB_EXIT
        jump = done & (in3 < n_items) & (nxt < n_items) & (qi_tab[nxt] == tile3)
        ptr = jnp.where(jump, ts_tab[tile3 + 1], nxt)
        return (ptr, in1, in2, in3, carries[0], carries[1], accs[0], accs[1])

    def body(st):
        return advance(1, advance(0, st))

    def pending(st):
        ptr, in2, in3, in4 = st[:4]
        return (ptr < n_items) | (in2 < n_items) | (in3 < n_items) | (in4 < n_items)

    zeros = jnp.zeros((tq, LANES), F32)
    filler = jnp.int32(n_items)
    lax.while_loop(pending, body, (jnp.int32(0), filler, filler, filler, zeros, zeros, zeros, zeros))


def sb_attention(proj, q_gain, k_gain, *, batch, seq, tq, tk):
    tq = min(tq, seq)
    tk = min(tk, tq)
    assert seq % tq == 0 and tq % tk == 0
    n_pairs = SB_HEADS // 2
    n_items, qi, kj, tile_start = _sb_items(seq // tq, tq // tk)
    g2 = lambda g: jnp.concatenate([g, g]).reshape(1, LANES).astype(F32)
    seq_blk = lambda s0: pl.BlockSpec((1, seq, LANES), lambda b, p, qt, kt, ts: (s0 + p, b, 0))
    stage = lambda dt: pltpu.VMEM((2, 2, tq, tk), dt)
    return pl.pallas_call(
        functools.partial(_sb_kernel, tq=tq, tk=tk, seq=seq, n_items=n_items),
        grid_spec=pltpu.PrefetchScalarGridSpec(
            num_scalar_prefetch=3,
            grid=(batch, n_pairs),
            in_specs=[
                seq_blk(SLAB_SB_Q), seq_blk(SLAB_SB_K), seq_blk(SLAB_SB_V),
                pl.BlockSpec((1, LANES), lambda b, p, qt, kt, ts: (0, 0)),
                pl.BlockSpec((1, LANES), lambda b, p, qt, kt, ts: (0, 0)),
            ],
            out_specs=pl.BlockSpec((1, seq, LANES), lambda b, p, qt, kt, ts: (p, b, 0)),
            scratch_shapes=[
                pltpu.VMEM((2, seq, LANES), BF16),
                pltpu.VMEM((seq, LANES), BF16),
                stage(F32), stage(BF16), stage(BF16), stage(BF16),
                pltpu.VMEM((2, 2, tq, LANES), F32),
            ],
        ),
        out_shape=jax.ShapeDtypeStruct((n_pairs, batch * seq, LANES), BF16),
        compiler_params=_params("parallel", "parallel"),
        name="sb_attention",
    )(jnp.asarray(qi, jnp.int32), jnp.asarray(kj, jnp.int32), jnp.asarray(tile_start, jnp.int32), proj, proj, proj, g2(q_gain), g2(k_gain))


GLA_ROWS = 256
GLA_BLOCKS_PER_STEP = 2
GLA_KW = GLA_HEADS * GLA_DK


def _gla_kernel(q_ref, k_ref, v_ref, a_ref, r_ref, aup_ref, ab_ref, og_ref, o_ref, state_ref):
    @pl.when(pl.program_id(1) == 0)
    def _():
        state_ref[...] = jnp.zeros_like(state_ref)

    state = state_ref[...]
    for blk in range(o_ref.shape[1] // GLA_ROWS):
        state = _gla_block(blk * GLA_ROWS, state, q_ref, k_ref, v_ref, a_ref, r_ref,
                           aup_ref, ab_ref, og_ref, o_ref)
    state_ref[...] = state


def _gla_block(b0, state, q_ref, k_ref, v_ref, a_ref, r_ref, aup_ref, ab_ref, og_ref, o_ref):
    rows = GLA_ROWS
    n_chunks = rows // GLA_CHUNK
    blk = pl.ds(b0, rows)

    pre = jnp.dot(a_ref[0, blk, :], aup_ref[...], preferred_element_type=F32) + ab_ref[...]
    log_a = -_softplus(-pre) * (1.0 / GLA_GATE_TAU)
    la_hi, la_lo = _split_bf16(log_a)

    ri = lax.broadcasted_iota(jnp.int32, (rows, rows), 0)
    ci = lax.broadcasted_iota(jnp.int32, (rows, rows), 1)
    same_chunk = (ri // GLA_CHUNK) == (ci // GLA_CHUNK)
    tri = same_chunk & (ri >= ci)
    tri_b = jnp.where(tri, 1.0, 0.0).astype(BF16)
    ones_b = jnp.where(same_chunk, 1.0, 0.0).astype(BF16)
    cum = (jnp.dot(tri_b, la_hi, preferred_element_type=F32)
           + jnp.dot(tri_b, la_lo, preferred_element_type=F32))
    cum_last = (jnp.dot(ones_b, la_hi, preferred_element_type=F32)
                + jnp.dot(ones_b, la_lo, preferred_element_type=F32))

    q_all = jnp.concatenate([q_ref[0, blk, :], q_ref[1, blk, :]], axis=-1).astype(F32)
    k_all = jnp.concatenate([k_ref[0, blk, :], k_ref[1, blk, :]], axis=-1).astype(F32)
    q_dec = q_all * (GLA_DK ** -0.5) * jnp.exp(cum)
    k_inv = (k_all * jnp.exp(-cum)).astype(BF16)
    k_end = (k_all * jnp.exp(cum_last - cum)).astype(BF16)
    v_heads = [v_ref[h, blk, :] for h in range(GLA_HEADS)]
    v_all = jnp.concatenate(v_heads, axis=-1)

    lane = lax.broadcasted_iota(jnp.int32, (rows, GLA_KW), 1)
    q_heads = [jnp.where((lane // GLA_DK) == h, q_dec, 0.0).astype(BF16) for h in range(GLA_HEADS)]

    o_heads = []
    for h in range(GLA_HEADS):
        sc = lax.dot_general(q_heads[h], k_inv, (((1,), (1,)), ((), ())), preferred_element_type=F32)
        sc = jnp.where(tri, sc, 0.0).astype(BF16)
        o_heads.append(jnp.dot(sc, v_heads[h], preferred_element_type=F32))

    ones_cols = jnp.ones((GLA_CHUNK, LANES), BF16)
    inter =[[] for _ in range(GLA_HEADS)]
    for c in range(n_chunks):
        r0, r1 = c * GLA_CHUNK, (c + 1) * GLA_CHUNK
        state_b = state.astype(BF16)
        for h in range(GLA_HEADS):
            inter[h].append(jnp.dot(q_heads[h][r0:r1], state_b, preferred_element_type=F32))
        kv = lax.dot_general(k_end[r0:r1], v_all[r0:r1], (((0,), (0,)), ((), ())),
                             preferred_element_type=F32)
        kv_sel = jnp.concatenate(
            [kv[h * GLA_DK:(h + 1) * GLA_DK, h * GLA_DV:(h + 1) * GLA_DV] for h in range(GLA_HEADS)], axis=0)
        tot = (lax.dot_general(la_hi[r0:r1], ones_cols, (((0,), (0,)), ((), ())), preferred_element_type=F32)
               + lax.dot_general(la_lo[r0:r1], ones_cols, (((0,), (0,)), ((), ())), preferred_element_type=F32))
        state = jnp.exp(tot) * state + kv_sel

    for h in range(GLA_HEADS):
        o = o_heads[h] + jnp.concatenate(inter[h], axis=0)
        o = _rms_rows(o) * og_ref[...]
        r = r_ref[h, blk, :].astype(F32)
        o_ref[h, blk, :] = (o * (r * jax.nn.sigmoid(r))).astype(o_ref.dtype)
    return state


def gla(proj, a_up, a_b, o_gain, *, batch, seq):
    rows = GLA_ROWS * GLA_BLOCKS_PER_STEP
    if seq % rows:
        rows = GLA_ROWS
    assert seq % rows == 0
    nb = seq // rows
    a_up_p = jnp.zeros((LANES, GLA_KW), BF16).at[:GLA_GATE_RANK].set(a_up.astype(BF16))
    row_blk = lambda n, s0: pl.BlockSpec((n, rows, LANES), lambda b, i: (s0 // n, b * nb + i, 0))
    return pl.pallas_call(
        _gla_kernel,
        grid=(batch, nb),
        in_specs=[
            row_blk(2, SLAB_GLA_Q), row_blk(2, SLAB_GLA_K), row_blk(4, SLAB_GLA_V),
            row_blk(1, SLAB_GLA_A), row_blk(4, SLAB_GLA_R),
            pl.BlockSpec((LANES, GLA_KW), lambda b, i: (0, 0)),
            pl.BlockSpec((1, GLA_KW), lambda b, i: (0, 0)),
            pl.BlockSpec((1, GLA_DV), lambda b, i: (0, 0)),
        ],
        out_specs=pl.BlockSpec((GLA_HEADS, rows, LANES), lambda b, i: (0, b * nb + i, 0)),
        out_shape=jax.ShapeDtypeStruct((GLA_HEADS, batch * seq, LANES), BF16),
        scratch_shapes=[pltpu.VMEM((GLA_KW, GLA_DV), F32)],
        compiler_params=_params("parallel", "arbitrary"),
        name="gla",
    )(proj, proj, proj, proj, proj, a_up_p,
      a_b.reshape(1, GLA_KW).astype(F32), o_gain.reshape(1, GLA_DV).astype(F32))


def _merge_kernel(x_ref, oa_ref, ob_ref, cq_ref, gate_ref, mkv_ref, cqg_ref, ckg_ref,
                  wsb_ref, wgla_ref, wca_ref, wout_ref, o_ref):
    ca_w = CA_HEADS * CA_HEAD_DIM
    oc = []
    for h in range(CA_HEADS):
        q = _rms_rows(cq_ref[h].astype(F32)) * (cqg_ref[...] * CA_HEAD_DIM ** -0.5)
        kk = _rms_rows(mkv_ref[:, h * CA_HEAD_DIM:(h + 1) * CA_HEAD_DIM].astype(F32)) * ckg_ref[...]
        s = lax.dot_general(q.astype(BF16), kk.astype(BF16), (((1,), (1,)), ((), ())),
                            preferred_element_type=F32)
        e = jnp.exp(s - jnp.max(s, axis=-1, keepdims=True))
        p = e * (1.0 / jnp.sum(e, axis=-1, keepdims=True))
        vv = mkv_ref[:, ca_w + h * CA_HEAD_DIM: ca_w + (h + 1) * CA_HEAD_DIM]
        oc.append(jnp.dot(p.astype(BF16), vv, preferred_element_type=F32).astype(BF16))
    o_c = jnp.concatenate(oc, axis=-1)
    o_a = jnp.concatenate([oa_ref[s] for s in range(4)], axis=-1)
    o_b = jnp.concatenate([ob_ref[s] for s in range(4)], axis=-1)

    def gate(b):
        g = jnp.concatenate([gate_ref[b * 8 + s] for s in range(8)], axis=-1).astype(F32)
        return jax.nn.sigmoid(g)

    merged = gate(0) * jnp.dot(o_a, wsb_ref[...], preferred_element_type=F32)
    merged += gate(1) * jnp.dot(o_b, wgla_ref[...], preferred_element_type=F32)
    merged += gate(2) * jnp.dot(o_c, wca_ref[...], preferred_element_type=F32)
    o_ref[...] = x_ref[...] + jnp.dot(merged.astype(BF16), wout_ref[...], preferred_element_type=F32)


def merge_branches(x, o_a, o_b, proj, mkv, cq_gain, ck_gain, w_sb, w_gla, w_ca, w_out, *, batch, seq, tm):
    n, d = x.shape
    tm = min(tm, seq)
    assert seq % tm == 0
    nt = seq // tm
    mem_len = mkv.shape[0] // batch
    full = lambda a: pl.BlockSpec(a.shape, lambda i: (0,) * a.ndim)
    cqg = cq_gain.reshape(1, CA_HEAD_DIM).astype(F32)
    ckg = ck_gain.reshape(1, CA_HEAD_DIM).astype(F32)
    return pl.pallas_call(
        _merge_kernel,
        grid=(n // tm,),
        in_specs=[
            pl.BlockSpec((tm, d), lambda i: (i, 0)),
            pl.BlockSpec((4, tm, LANES), lambda i: (0, i, 0)),
            pl.BlockSpec((4, tm, LANES), lambda i: (0, i, 0)),
            pl.BlockSpec((4, tm, LANES), lambda i: (SLAB_CA_Q // 4, i, 0)),
            pl.BlockSpec((24, tm, LANES), lambda i: (SLAB_GATES // 24, i, 0)),
            pl.BlockSpec((mem_len, mkv.shape[1]), lambda i: (i // nt, 0)),
            full(cqg), full(ckg), full(w_sb), full(w_gla), full(w_ca), full(w_out),
        ],
        out_specs=pl.BlockSpec((tm, d), lambda i: (i, 0)),
        out_shape=jax.ShapeDtypeStruct((n, d), F32),
        compiler_params=_params("parallel"),
        name="merge_branches",
    )(x, o_a, o_b, proj, proj, mkv,
      cqg, ckg, w_sb, w_gla, w_ca, w_out)


def _ffn_kernel(x_ref, g_ref, wg_ref, wu_ref, wd_ref, o_ref, hn_ref, acc_ref):
    f = pl.program_id(1)

    @pl.when(f == 0)
    def _():
        hn_ref[...] = (_rms_rows(x_ref[...]) * g_ref[...]).astype(BF16)
        acc_ref[...] = jnp.zeros_like(acc_ref)

    h = hn_ref[...]
    gt = jnp.dot(h, wg_ref[...], preferred_element_type=F32)
    up = jnp.dot(h, wu_ref[...], preferred_element_type=F32)
    act = (gt * jax.nn.sigmoid(gt) * up).astype(BF16)
    acc_ref[...] += jnp.dot(act, wd_ref[...], preferred_element_type=F32)

    @pl.when(f == pl.num_programs(1) - 1)
    def _():
        o_ref[...] = x_ref[...] + acc_ref[...]


def ffn_dense(x, g, w_gate_up, w_down, *, tm, tf):
    n, d = x.shape
    d_ff = w_down.shape[0]
    tm = min(tm, n)
    assert n % tm == 0 and d_ff % tf == 0 and tf % LANES == 0
    nf = d_ff // tf
    return pl.pallas_call(
        _ffn_kernel,
        grid=(n // tm, nf),
        in_specs=[
            pl.BlockSpec((tm, d), lambda i, f: (i, 0)),
            pl.BlockSpec((1, d), lambda i, f: (0, 0)),
            pl.BlockSpec((d, tf), lambda i, f: (0, f)),
            pl.BlockSpec((d, tf), lambda i, f: (0, nf + f)),
            pl.BlockSpec((tf, d), lambda i, f: (f, 0)),
        ],
        out_specs=pl.BlockSpec((tm, d), lambda i, f: (i, 0)),
        out_shape=jax.ShapeDtypeStruct((n, d), F32),
        scratch_shapes=[pltpu.VMEM((tm, d), BF16), pltpu.VMEM((tm, d), F32)],
        compiler_params=_params("parallel", "arbitrary"),
        name="ffn_dense",
    )(x, g.reshape(1, d), w_gate_up, w_gate_up, w_down)


def _router_kernel(x_ref, g_ref, wr_ref, c_ref):
    hn = _rms_rows(x_ref[...]) * g_ref[...]
    logits = jnp.dot(hn, wr_ref[...], preferred_element_type=F32, precision=lax.Precision.HIGHEST)
    lane = lax.broadcasted_iota(jnp.int32, logits.shape, 1).astype(F32)
    neg = jnp.float32(-jnp.inf)
    logits = jnp.where(lane < N_EXPERTS, logits, neg)
    m1 = jnp.max(logits, axis=-1, keepdims=True)
    i1 = jnp.min(jnp.where(logits == m1, lane, float(LANES)), axis=-1, keepdims=True)
    rest = jnp.where(lane == i1, neg, logits)
    m2 = jnp.max(rest, axis=-1, keepdims=True)
    i2 = jnp.min(jnp.where(rest == m2, lane, float(LANES)), axis=-1, keepdims=True)
    e2 = jnp.exp(m2 - m1)
    inv = 1.0 / (1.0 + e2)
    c_ref[...] = (jnp.where(lane == 0.0, inv, 0.0) + jnp.where(lane == 1.0, e2 * inv, 0.0)
                  + jnp.where(lane == 2.0, i1, 0.0) + jnp.where(lane == 3.0, i2, 0.0))


def router(x, g, w_router, *, tm):
    n, d = x.shape
    tm = min(tm, n)
    wr = jnp.zeros((d, LANES), F32).at[:, :N_EXPERTS].set(w_router)
    return pl.pallas_call(
        _router_kernel,
        grid=(n // tm,),
        in_specs=[
            pl.BlockSpec((tm, d), lambda i: (i, 0)),
            pl.BlockSpec((1, d), lambda i: (0, 0)),
            pl.BlockSpec((d, LANES), lambda i: (0, 0)),
        ],
        out_specs=pl.BlockSpec((tm, LANES), lambda i: (i, 0)),
        out_shape=jax.ShapeDtypeStruct((n, LANES), F32),
        compiler_params=_params("parallel"),
        name="router",
    )(x, g.reshape(1, d), wr)


def _row_copy(src_hbm, row, dst_vmem, r, sem):
    return pltpu.make_async_copy(src_hbm.at[pl.ds(row, 1)], dst_vmem.at[pl.ds(r, 1)], sem)


def _expert_ffn_kernel(te_ref, nu_ref, cur_ref, nxt_ref, x_hbm, g_ref, wg_ref, wu_ref, wd_ref, o_ref,
                       xbuf_ref, sem, hn_ref, acc_ref, *, nf):
    i = pl.program_id(0)
    f = pl.program_id(1)
    rows = o_ref.shape[0]
    used = i < nu_ref[0]
    slot = i % 2

    def normalise(s):
        hn_ref[s] = (_rms_rows(xbuf_ref[...]) * g_ref[...]).astype(BF16)

    @pl.when((i == 0) & (f == 0))
    def _():
        def start(r, carry):
            _row_copy(x_hbm, cur_ref[0, 0, r], xbuf_ref, r, sem).start()
            return carry
        lax.fori_loop(0, rows, start, 0, unroll=GATHER_UNROLL)

        def wait(r, carry):
            _row_copy(x_hbm, 0, xbuf_ref, r, sem).wait()
            return carry
        lax.fori_loop(0, rows, wait, 0, unroll=GATHER_UNROLL)
        normalise(0)

    def ffn_step(first, last):
        if first:
            for r in range(rows):
                _row_copy(x_hbm, nxt_ref[0, 0, r], xbuf_ref, r, sem).start()
        h = hn_ref[slot]
        gt = jnp.dot(h, wg_ref[0], preferred_element_type=F32)
        up = jnp.dot(h, wu_ref[0], preferred_element_type=F32)
        act = (gt * jax.nn.sigmoid(gt) * up).astype(BF16)
        part = jnp.dot(act, wd_ref[0], preferred_element_type=F32)
        total = part if first else acc_ref[...] + part
        if last:
            o_ref[...] = total
            for r in range(rows):
                _row_copy(x_hbm, 0, xbuf_ref, r, sem).wait()
            normalise(1 - slot)
        else:
            acc_ref[...] = total

    if nf == 1:
        pl.when(used)(lambda: ffn_step(True, True))
    else:
        pl.when(used & (f == 0))(lambda: ffn_step(True, False))
        pl.when(used & (f == nf - 1))(lambda: ffn_step(False, True))
        if nf > 2:
            pl.when(used & (f > 0) & (f < nf - 1))(lambda: ffn_step(False, False))

    @pl.when(jnp.logical_not(used) & (f == nf - 1))
    def _():
        o_ref[...] = jnp.zeros_like(o_ref)


def expert_ffn(x, src, g, tile_expert, n_used, w_gate_up, w_down, *, tm, tf):
    rows = src.shape[0]
    d = x.shape[1]
    d_ff = w_down.shape[1]
    assert rows % tm == 0 and d_ff % tf == 0 and tf % LANES == 0
    nf = d_ff // tf
    n_tiles = rows // tm
    src = src.reshape(n_tiles, 1, tm)
    return pl.pallas_call(
        functools.partial(_expert_ffn_kernel, nf=nf),
        grid_spec=pltpu.PrefetchScalarGridSpec(
            num_scalar_prefetch=2,
            grid=(n_tiles, nf),
            in_specs=[
                pl.BlockSpec((1, 1, tm), lambda i, f, te, nu: (i, 0, 0), memory_space=pltpu.SMEM),
                pl.BlockSpec((1, 1, tm), lambda i, f, te, nu: (jnp.minimum(i + 1, n_tiles - 1), 0, 0),
                             memory_space=pltpu.SMEM),
                pl.BlockSpec(memory_space=pl.ANY),
                pl.BlockSpec((1, d), lambda i, f, te, nu: (0, 0)),
                pl.BlockSpec((1, d, tf), lambda i, f, te, nu: (te[i], 0, f)),
                pl.BlockSpec((1, d, tf), lambda i, f, te, nu: (te[i], 0, nf + f)),
                pl.BlockSpec((1, tf, d), lambda i, f, te, nu: (te[i], f, 0)),
            ],
            out_specs=pl.BlockSpec((tm, d), lambda i, f, te, nu: (i, 0)),
            scratch_shapes=[pltpu.VMEM((tm, d), F32), pltpu.SemaphoreType.DMA,
                            pltpu.VMEM((2, tm, d), BF16), pltpu.VMEM((tm, d), F32)],
        ),
        out_shape=jax.ShapeDtypeStruct((rows, d), F32),
        compiler_params=_params("arbitrary", "arbitrary"),
        name="expert_ffn",
    )(tile_expert, n_used, src, src, x, g.reshape(1, d), w_gate_up, w_gate_up, w_down)


def _combine_kernel(p1_ref, p2_ref, n1_ref, n2_ref, x_ref, rw_ref, ys_hbm, o_ref, a_ref, b_ref, sem):
    i = pl.program_id(0)
    rows = o_ref.shape[0]
    slot = i % 2

    def issue(q1_ref, q2_ref, s):
        def one(r, carry):
            _row_copy(ys_hbm, q1_ref[0, 0, r], a_ref.at[s], r, sem.at[0, s]).start()
            _row_copy(ys_hbm, q2_ref[0, 0, r], b_ref.at[s], r, sem.at[1, s]).start(priority=1)
            return carry
        lax.fori_loop(0, rows, one, 0, unroll=GATHER_UNROLL)

    @pl.when(i == 0)
    def _():
        issue(p1_ref, p2_ref, 0)

    @pl.when(i + 1 < pl.num_programs(0))
    def _():
        issue(n1_ref, n2_ref, 1 - slot)

    def drain(r, carry):
        _row_copy(ys_hbm, 0, a_ref.at[slot], r, sem.at[0, slot]).wait()
        _row_copy(ys_hbm, 0, b_ref.at[slot], r, sem.at[1, slot]).wait()
        return carry
    lax.fori_loop(0, rows, drain, 0, unroll=GATHER_UNROLL)

    rw = rw_ref[...]
    o_ref[...] = x_ref[...] + (rw[:, 0:1] * a_ref[slot] + rw[:, 1:2] * b_ref[slot])


def moe_combine(x, route, ys, pos1, pos2, *, tc):
    n, d = x.shape
    assert n % tc == 0
    nt = n // tc
    idx_spec = pl.BlockSpec((1, 1, tc), lambda i: (i, 0, 0), memory_space=pltpu.SMEM)
    nxt_spec = pl.BlockSpec((1, 1, tc), lambda i: (jnp.minimum(i + 1, nt - 1), 0, 0), memory_space=pltpu.SMEM)
    pos1 = pos1.reshape(nt, 1, tc)
    pos2 = pos2.reshape(nt, 1, tc)
    return pl.pallas_call(
        _combine_kernel,
        grid=(nt,),
        in_specs=[
            idx_spec, idx_spec, nxt_spec, nxt_spec,
            pl.BlockSpec((tc, d), lambda i: (i, 0)),
            pl.BlockSpec((tc, LANES), lambda i: (i, 0)),
            pl.BlockSpec(memory_space=pl.ANY),
        ],
        out_specs=pl.BlockSpec((tc, d), lambda i: (i, 0)),
        out_shape=jax.ShapeDtypeStruct((n, d), F32),
        scratch_shapes=[pltpu.VMEM((2, tc, d), F32), pltpu.VMEM((2, tc, d), F32),
                        pltpu.SemaphoreType.DMA((2, 2))],
        compiler_params=_params("arbitrary"),
        name="moe_combine",
    )(pos1, pos2, pos1, pos2, x, route, ys)


def _dispatch_plan(route, *, tm):
    n = route.shape[0]
    ids = route[:, 2:4].astype(jnp.int32)
    e_flat = ids.T.reshape(-1)
    onehot = (e_flat[:, None] == jnp.arange(N_EXPERTS)[None, :]).astype(jnp.int32)
    running = jnp.cumsum(onehot, axis=0)
    rank = jnp.sum(running * onehot, axis=1) - 1
    counts = running[-1]
    padded = ((counts + tm - 1) // tm) * tm
    ends = jnp.cumsum(padded)
    starts = ends - padded
    pos = starts[e_flat] + rank
    rows = TOP_K * n + N_EXPERTS * tm
    tok = jnp.tile(jnp.arange(n, dtype=jnp.int32), TOP_K)
    src = jnp.zeros((rows,), jnp.int32).at[pos].set(tok)
    tile_start = jnp.arange(rows // tm, dtype=jnp.int32) * tm
    tile_expert = jnp.minimum(jnp.sum(tile_start[:, None] >= ends[None, :], axis=1), N_EXPERTS - 1)
    n_used = (ends[-1] // tm).reshape(1).astype(jnp.int32)
    return src, tile_expert.astype(jnp.int32), n_used, pos[:n].astype(jnp.int32), pos[n:].astype(jnp.int32)


def ffn_moe(x, g, route, w_gate_up, w_down, *, tm, tf, tc):
    src, tile_expert, n_used, pos1, pos2 = _dispatch_plan(route, tm=tm)
    ys = expert_ffn(x, src, g, tile_expert, n_used, w_gate_up, w_down, tm=tm, tf=tf)
    return moe_combine(x, route, ys, pos1, pos2, tc=min(tc, x.shape[0]))


def _slab_weights(w_in):
    d = w_in.shape[0]
    sbw = SB_HEADS * SB_HEAD_DIM
    glv = GLA_HEADS * GLA_DV
    caw = CA_HEADS * CA_HEAD_DIM
    sizes = (sbw, sbw, sbw, GLA_KW, GLA_KW, glv, GLA_GATE_RANK, glv, caw, N_BRANCH * D_MODEL)
    parts, off = [], 0
    for s in sizes:
        parts.append(w_in[:, off:off + s])
        off += s
    sb_q, sb_k, sb_v, g_q, g_k, g_v, g_a, g_r, ca_q, gates = parts
    pad = jnp.zeros((d, 2 * LANES - GLA_GATE_RANK), w_in.dtype)
    cols = [sb_q, sb_k, sb_v, g_v, g_r, ca_q, gates, g_q, g_k, g_a, pad]
    w = jnp.concatenate([c.astype(BF16) for c in cols], axis=1)
    assert w.shape[1] == N_SLABS * LANES
    return w


def kernel(x, mem, mix_norm_g, w_in, sb_q_g, sb_k_g, gla_a_up, gla_a_b, gla_o_g, ca_q_g, ca_k_g, mem_norm_g, w_mem_kv, w_br_sb, w_br_gla, w_br_ca, w_out, ffn_norm_g, ff_w_gate_up, ff_w_down, moe_w_router, moe_w_gate_up, moe_w_down):
    batch, seq, d = x.shape
    depth = w_in.shape[0]
    n = batch * seq
    xf = x.reshape(n, d)
    memf = mem.reshape(-1, d)
    bf = lambda a: a.astype(BF16)

    t = TILES
    for layer in range(depth):
        proj = norm_matmul(xf, mix_norm_g[layer], _slab_weights(w_in[layer]), **t["in_proj"], slabs=True)
        mkv = norm_matmul(memf, mem_norm_g[layer], bf(w_mem_kv[layer]), **t["mem_kv"], slabs=False)
        o_a = sb_attention(proj, sb_q_g[layer], sb_k_g[layer], batch=batch, seq=seq, **t["sb"])
        o_b = gla(proj, gla_a_up[layer], gla_a_b[layer], gla_o_g[layer], batch=batch, seq=seq)
        xf = merge_branches(xf, o_a, o_b, proj, mkv, ca_q_g[layer], ca_k_g[layer],
                            bf(w_br_sb[layer]), bf(w_br_gla[layer]), bf(w_br_ca[layer]), bf(w_out[layer]),
                            batch=batch, seq=seq, **t["merge"])
        j = layer // 2
        if layer % 2 == 0:
            xf = ffn_dense(xf, ffn_norm_g[layer], bf(ff_w_gate_up[j]), bf(ff_w_down[j]), **t["ffn"])
        else:
            route = router(xf, ffn_norm_g[layer], moe_w_router[j], **t["router"])
            xf = ffn_moe(xf, ffn_norm_g[layer], route, bf(moe_w_gate_up[j]), bf(moe_w_down[j]), **t["moe"])
    return xf.reshape(batch, seq, d)
```
